```python
import jax, jax.numpy as jnp
from jax import lax
import numpy as np

D_MODEL = 1024
BATCH = 8
SEQ = 4096
DEPTH = 4

N_MIXERS = 2
HEAD_DIM = 64
FOX_HEADS = D_MODEL // HEAD_DIM
DIL_HEADS = D_MODEL // HEAD_DIM
DIL_CONFIGS = ((128, 1), (512, 4), (2048, 16))
N_DIL_GROUPS = len(DIL_CONFIGS)
ROPE_THETA = 500000.0
ROPE_DIM = HEAD_DIM // 4
D_FF = 2816
Q_BLOCK = 128
EPS = 1e-6
MACARON_WEIGHT = 0.5
N_SUBLAYERS = 3
N_FOX_LAYERS = (DEPTH + 1) // 2
N_DIL_LAYERS = DEPTH // 2
FOX_IN = 3 * FOX_HEADS * HEAD_DIM + FOX_HEADS
DIL_IN = N_DIL_GROUPS * 3 * DIL_HEADS * HEAD_DIM

kernel_name = "hybrid_fox_dilated_macaron_adaln"


def rms_norm(x, g):
    xf = x.astype(jnp.float32)
    y = xf * lax.rsqrt(jnp.mean(xf * xf, axis=-1, keepdims=True) + EPS)
    return (y * g.astype(jnp.float32)).astype(x.dtype)


def modulate(h, shift, scale):
    return h * (1.0 + scale[:, None, :]) + shift[:, None, :]


def swiglu(h, w_gate, w_up, w_down):
    return (jax.nn.silu(h @ w_gate) * (h @ w_up)) @ w_down


def rope_tables(positions):
    half = ROPE_DIM // 2
    inv_freq = ROPE_THETA ** (-(jnp.arange(half, dtype=jnp.float32) * 2.0 / ROPE_DIM))
    ang = positions.astype(jnp.float32)[..., None] * inv_freq
    return jnp.cos(ang)[:, :, None, :], jnp.sin(ang)[:, :, None, :]


def apply_rope(x, cos, sin):
    half = ROPE_DIM // 2
    xr = x[..., :ROPE_DIM].astype(jnp.float32)
    x1, x2 = xr[..., :half], xr[..., half:]
    rot = jnp.concatenate([x1 * cos - x2 * sin, x2 * cos + x1 * sin], axis=-1).astype(x.dtype)
    return jnp.concatenate([rot, x[..., ROPE_DIM:]], axis=-1)


def fox_attention(h, w_in, b_f, q_gain, k_gain, w_out):
    B, S, _ = h.shape
    H, hd = FOX_HEADS, HEAD_DIM
    proj = h @ w_in
    qkv = proj[..., : 3 * H * hd].reshape(B, S, 3, H, hd)
    f_logit = proj[..., 3 * H * hd:].astype(jnp.float32) + b_f.astype(jnp.float32)
    q = rms_norm(qkv[:, :, 0], q_gain).transpose(0, 2, 1, 3)
    k = rms_norm(qkv[:, :, 1], k_gain).transpose(0, 2, 1, 3)
    v = qkv[:, :, 2].transpose(0, 2, 1, 3)
    cum = jnp.cumsum(jax.nn.log_sigmoid(f_logit), axis=1).transpose(0, 2, 1)
    scale = hd ** -0.5
    outs = []
    for blk in range(S // Q_BLOCK):
        q0 = blk * Q_BLOCK
        kv_len = q0 + Q_BLOCK
        logits = jnp.einsum("bhqd,bhkd->bhqk", q[:, :, q0:kv_len], k[:, :, :kv_len]).astype(jnp.float32) * scale
        logits = logits + cum[:, :, q0:kv_len, None] - cum[:, :, None, :kv_len]
        q_idx = q0 + jnp.arange(Q_BLOCK)
        k_idx = jnp.arange(kv_len)
        mask = k_idx[None, :] <= q_idx[:, None]
        logits = jnp.where(mask[None, None], logits, -jnp.inf)
        p = jax.nn.softmax(logits, axis=-1).astype(v.dtype)
        outs.append(jnp.einsum("bhqk,bhkd->bhqd", p, v[:, :, :kv_len]))
    o = jnp.concatenate(outs, axis=2).transpose(0, 2, 1, 3).reshape(B, S, H * hd)
    return o @ w_out


def dilated_group(q, k, v, window, dilation):
    B, S, H, hd = q.shape
    n_keys = window // dilation + 1
    offsets = dilation * jnp.arange(n_keys)
    scale = hd ** -0.5

    def block_fn(q0):
        qb = lax.dynamic_slice_in_dim(q, q0, Q_BLOCK, axis=1)
        t = q0 + jnp.arange(Q_BLOCK)
        idx = t[:, None] - offsets[None, :]
        valid = idx >= 0
        idx_c = jnp.maximum(idx, 0)
        kb = jnp.take(k, idx_c, axis=1)
        vb = jnp.take(v, idx_c, axis=1)
        logits = jnp.einsum("bqhd,bqkhd->bhqk", qb, kb).astype(jnp.float32) * scale
        logits = jnp.where(valid[None, None], logits, -jnp.inf)
        lse = jax.nn.logsumexp(logits, axis=-1)
        p = jnp.exp(logits - lse[..., None]).astype(v.dtype)
        ob = jnp.einsum("bhqk,bqkhd->bqhd", p, vb)
        return ob, lse

    starts = jnp.arange(S // Q_BLOCK) * Q_BLOCK
    o, lse = lax.map(block_fn, starts)
    o = o.transpose(1, 0, 2, 3, 4).reshape(B, S, H, hd)
    lse = lse.transpose(1, 0, 3, 2).reshape(B, S, H)
    return o, lse


def dilated_attention(h, cos, sin, w_in, q_gain, k_gain, w_out):
    B, S, _ = h.shape
    H, hd = DIL_HEADS, HEAD_DIM
    proj = (h @ w_in).reshape(B, S, N_DIL_GROUPS, 3, H, hd)
    outs, lses = [], []
    for g, (window, dilation) in enumerate(DIL_CONFIGS):
        q = apply_rope(rms_norm(proj[:, :, g, 0], q_gain[g]), cos, sin)
        k = apply_rope(rms_norm(proj[:, :, g, 1], k_gain[g]), cos, sin)
        o, lse = dilated_group(q, k, proj[:, :, g, 2], window, dilation)
        outs.append(o)
        lses.append(lse)
    alpha = jax.nn.softmax(jnp.stack(lses, axis=0), axis=0)
    o = jnp.sum(alpha[..., None] * jnp.stack(outs, axis=0).astype(jnp.float32), axis=0).astype(h.dtype)
    return o.reshape(B, S, H * hd) @ w_out


def setup_inputs(seed: int = 0) -> dict:
    key = jax.random.key(seed)
    ks = jax.random.split(key, 20)
    D = D_MODEL
    nrm = jax.random.normal
    x = nrm(ks[0], (BATCH, SEQ, D), jnp.float32)
    c = nrm(ks[1], (BATCH, D), jnp.float32)
    offs = jax.random.randint(ks[2], (BATCH, 1), 0, 4096, dtype=jnp.int32)
    positions = (offs + jnp.arange(SEQ, dtype=jnp.int32)[None, :]).astype(jnp.int32)
    mod_w = nrm(ks[3], (DEPTH, D, N_SUBLAYERS * 3 * D), jnp.float32) * (0.5 * D ** -0.5)
    mod_b = nrm(ks[4], (DEPTH, N_SUBLAYERS * 3 * D), jnp.float32) * 0.01
    norm_g = 1.0 + 0.02 * nrm(ks[5], (DEPTH, N_SUBLAYERS, D), jnp.float32)
    ffn_w_gate = nrm(ks[6], (DEPTH, 2, D, D_FF), jnp.float32) * D ** -0.5
    ffn_w_up = nrm(ks[7], (DEPTH, 2, D, D_FF), jnp.float32) * D ** -0.5
    ffn_w_down = nrm(ks[8], (DEPTH, 2, D_FF, D), jnp.float32) * D_FF ** -0.5
    fox_w_in = nrm(ks[9], (N_FOX_LAYERS, D, FOX_IN), jnp.float32) * D ** -0.5
    fox_b_f = jax.random.uniform(ks[10], (N_FOX_LAYERS, FOX_HEADS), jnp.float32, 1.0, 6.0)
    fox_q_g = 1.0 + 0.02 * nrm(ks[11], (N_FOX_LAYERS, HEAD_DIM), jnp.float32)
    fox_k_g = 1.0 + 0.02 * nrm(ks[12], (N_FOX_LAYERS, HEAD_DIM), jnp.float32)
    fox_w_out = nrm(ks[13], (N_FOX_LAYERS, FOX_HEADS * HEAD_DIM, D), jnp.float32) * (FOX_HEADS * HEAD_DIM) ** -0.5
    dil_w_in = nrm(ks[14], (N_DIL_LAYERS, D, DIL_IN), jnp.float32) * D ** -0.5
    dil_q_g = 1.0 + 0.02 * nrm(ks[15], (N_DIL_LAYERS, N_DIL_GROUPS, HEAD_DIM), jnp.float32)
    dil_k_g = 1.0 + 0.02 * nrm(ks[16], (N_DIL_LAYERS, N_DIL_GROUPS, HEAD_DIM), jnp.float32)
    dil_w_out = nrm(ks[17], (N_DIL_LAYERS, DIL_HEADS * HEAD_DIM, D), jnp.float32) * (DIL_HEADS * HEAD_DIM) ** -0.5
    return {"x": x, "c": c, "positions": positions, "mod_w": mod_w, "mod_b": mod_b,
            "norm_g": norm_g, "ffn_w_gate": ffn_w_gate, "ffn_w_up": ffn_w_up,
            "ffn_w_down": ffn_w_down, "fox_w_in": fox_w_in, "fox_b_f": fox_b_f,
            "fox_q_g": fox_q_g, "fox_k_g": fox_k_g, "fox_w_out": fox_w_out,
            "dil_w_in": dil_w_in, "dil_q_g": dil_q_g, "dil_k_g": dil_k_g, "dil_w_out": dil_w_out}


def reference(x, c, positions, mod_w, mod_b, norm_g, ffn_w_gate, ffn_w_up, ffn_w_down,
              fox_w_in, fox_b_f, fox_q_g, fox_k_g, fox_w_out,
              dil_w_in, dil_q_g, dil_k_g, dil_w_out):
    B = x.shape[0]
    D = x.shape[-1]
    cos, sin = rope_tables(positions)
    c_act = jax.nn.silu(c)
    for i in range(DEPTH):
        mod = (c_act @ mod_w[i] + mod_b[i]).reshape(B, N_SUBLAYERS, 3, D)
        shift, scale, gate = mod[:, :, 0], mod[:, :, 1], mod[:, :, 2]
        h = modulate(rms_norm(x, norm_g[i, 0]), shift[:, 0], scale[:, 0])
        x = x + MACARON_WEIGHT * gate[:, 0, None, :] * swiglu(h, ffn_w_gate[i, 0], ffn_w_up[i, 0], ffn_w_down[i, 0])
        h = modulate(rms_norm(x, norm_g[i, 1]), shift[:, 1], scale[:, 1])
        j = i // N_MIXERS
        if i % N_MIXERS == 0:
            y = fox_attention(h, fox_w_in[j], fox_b_f[j], fox_q_g[j], fox_k_g[j], fox_w_out[j])
        else:
            y = dilated_attention(h, cos, sin, dil_w_in[j], dil_q_g[j], dil_k_g[j], dil_w_out[j])
        x = x + gate[:, 1, None, :] * y
        h = modulate(rms_norm(x, norm_g[i, 2]), shift[:, 2], scale[:, 2])
        x = x + MACARON_WEIGHT * gate[:, 2, None, :] * swiglu(h, ffn_w_gate[i, 1], ffn_w_up[i, 1], ffn_w_down[i, 1])
    return x
```

```python
import functools

import jax
import jax.numpy as jnp
from jax import lax
from jax.experimental import pallas as pl
from jax.experimental.pallas import tpu as pltpu

HEAD_DIM = 64
ROPE_DIM = 16
ROPE_THETA = 500000.0
EPS = 1e-6
MACARON_WEIGHT = 0.5
N_SUBLAYERS = 3
DIL_CONFIGS = ((128, 1), (512, 4), (2048, 16))
DIL_WINDOW_KEYS = 128

LANES = 128
MXU_DIM = 256
VMEM_LIMIT = 56 * 1024 * 1024

F32 = jnp.float32
BF16 = jnp.bfloat16


def _params(semantics, vmem=VMEM_LIMIT):
    return pltpu.CompilerParams(dimension_semantics=semantics, vmem_limit_bytes=vmem)


def _resident(block_shape, index_map):
    return pl.BlockSpec(block_shape, index_map, pipeline_mode=pl.Buffered(1))


def _dot(a, b):
    return jnp.dot(a, b, preferred_element_type=F32)


def _dot_nt(a, b):
    return lax.dot_general(a, b, (((1,), (1,)), ((), ())), preferred_element_type=F32)


def _sigmoid(x):
    return 1.0 / (1.0 + jnp.exp(-x))


def _norm_mod(x, g, shift, scale):
    ms = jnp.mean(x * x, axis=-1, keepdims=True)
    y = x * lax.rsqrt(ms + EPS) * g
    return y * (1.0 + scale) + shift


def _head_block_ones(n):
    r = lax.broadcasted_iota(jnp.int32, (n, n), 0) // HEAD_DIM
    c = lax.broadcasted_iota(jnp.int32, (n, n), 1) // HEAD_DIM
    return jnp.where(r == c, 1.0, 0.0).astype(BF16)


def _head_rms_norm(x, gain):
    ones = _head_block_ones(MXU_DIM)
    outs = []
    for c in range(x.shape[1] // MXU_DIM):
        xc = x[:, c * MXU_DIM:(c + 1) * MXU_DIM]
        ssq = _dot((xc * xc).astype(BF16), ones)
        r = lax.rsqrt(ssq * (1.0 / HEAD_DIM) + EPS)
        outs.append(xc * r * gain[:, c * MXU_DIM:(c + 1) * MXU_DIM])
    return jnp.concatenate(outs, axis=1)


def _mod_kernel(c_ref, w_ref, b_ref, o_ref):
    c = c_ref[...]
    c_act = c * _sigmoid(c)
    o_ref[0] = jnp.dot(c_act, w_ref[0], preferred_element_type=F32,
                       precision=lax.Precision.HIGHEST) + b_ref[0]


def _modulation(c, mod_w, mod_b):
    depth, d, n = mod_w.shape
    b = c.shape[0]
    tn = 1024
    return pl.pallas_call(
        _mod_kernel,
        grid=(depth, n // tn),
        in_specs=[
            pl.BlockSpec((b, d), lambda i, j: (0, 0)),
            pl.BlockSpec((1, d, tn), lambda i, j: (i, 0, j)),
            pl.BlockSpec((1, 1, tn), lambda i, j: (i, 0, j)),
        ],
        out_specs=pl.BlockSpec((1, b, tn), lambda i, j: (i, 0, j)),
        out_shape=jax.ShapeDtypeStruct((depth, b, n), F32),
        compiler_params=_params(("arbitrary", "arbitrary")),
        name="modulation",
    )(c, mod_w, mod_b.reshape(depth, 1, n))


FFN_CHUNK = 256


def _ffn_kernel(x_ref, g_ref, sh_ref, sc_ref, gt_ref, wg_ref, wu_ref, wd_ref, o_ref):
    x = x_ref[0]
    h = _norm_mod(x, g_ref[...], sh_ref[0], sc_ref[0]).astype(BF16)
    acc = jnp.zeros_like(x)
    for c in range(wg_ref.shape[1] // FFN_CHUNK):
        cs = slice(c * FFN_CHUNK, (c + 1) * FFN_CHUNK)
        gg = _dot(h, wg_ref[:, cs])
        uu = _dot(h, wu_ref[:, cs])
        a = (gg * _sigmoid(gg) * uu).astype(BF16)
        acc = acc + _dot(a, wd_ref[cs, :])
    o_ref[0] = x + (MACARON_WEIGHT * gt_ref[0]) * acc


def _ffn(x, g, shift, scale, gate, wg, wu, wd, layer, which, tm=512):
    b, s, d = x.shape
    f = wg.shape[-1]
    row = pl.BlockSpec((1, 1, d), lambda i, j: (i, 0, 0))
    return pl.pallas_call(
        _ffn_kernel,
        grid=(b, s // tm),
        in_specs=[
            pl.BlockSpec((1, tm, d), lambda i, j: (i, j, 0)),
            pl.BlockSpec((1, d), lambda i, j: (0, 0)),
            row, row, row,
            _resident((None, None, d, f), lambda i, j: (layer, which, 0, 0)),
            _resident((None, None, d, f), lambda i, j: (layer, which, 0, 0)),
            _resident((None, None, f, d), lambda i, j: (layer, which, 0, 0)),
        ],
        out_specs=pl.BlockSpec((1, tm, d), lambda i, j: (i, j, 0)),
        out_shape=jax.ShapeDtypeStruct(x.shape, x.dtype),
        compiler_params=_params(("arbitrary", "arbitrary")),
        name="ffn",
    )(x, g, shift, scale, gate, wg, wu, wd)


def _cumsum_rows(x):
    n = x.shape[0]
    row = lax.broadcasted_iota(jnp.int32, x.shape, 0)
    shift = 1
    while shift < n:
        x = x + jnp.where(row >= shift, pltpu.roll(x, shift, 0), 0.0)
        shift *= 2
    return x


def _fox_in_kernel(x_ref, g_ref, sh_ref, sc_ref, w_ref, bf_ref, qg_ref, kg_ref,
                   q_ref, k_ref, v_ref, cum_ref, carry_ref):
    d = x_ref.shape[2]
    tm = x_ref.shape[1]

    @pl.when(pl.program_id(1) == 0)
    def _():
        carry_ref[...] = jnp.zeros_like(carry_ref)

    h = _norm_mod(x_ref[0], g_ref[...], sh_ref[0], sc_ref[0]).astype(BF16)
    q = _head_rms_norm(_dot(h, w_ref[:, 0:d]), qg_ref[...])
    q_ref[0] = (q * (HEAD_DIM ** -0.5)).astype(BF16)
    k = _head_rms_norm(_dot(h, w_ref[:, d:2 * d]), kg_ref[...])
    k_ref[0] = k.astype(BF16)
    v_ref[0] = _dot(h, w_ref[:, 2 * d:3 * d]).astype(BF16)

    z = _dot(h, w_ref[:, 3 * d:3 * d + LANES]) + bf_ref[...]
    logf = jnp.minimum(z, 0.0) - jnp.log1p(jnp.exp(-jnp.abs(z)))
    lane = lax.broadcasted_iota(jnp.int32, logf.shape, 1)
    logf = jnp.where(lane < bf_ref.shape[1] // 8, logf, 0.0)
    cum = _cumsum_rows(logf) + carry_ref[...]
    carry_ref[...] = cum[tm - 1:tm, :]
    cum_ref[0] = cum


def _fox_in(x, g, shift, scale, w_in, b_f, q_g, k_g, tm=512):
    b, s, d = x.shape
    n_heads = d // HEAD_DIM
    assert n_heads * 8 == LANES
    w = jnp.pad(w_in, ((0, 0), (0, LANES - n_heads))).astype(BF16)
    bf = jnp.pad(b_f, (0, LANES - n_heads)).reshape(1, LANES)
    qg = jnp.tile(q_g, n_heads).reshape(1, d)
    kg = jnp.tile(k_g, n_heads).reshape(1, d)
    row = pl.BlockSpec((1, 1, d), lambda i, j: (i, 0, 0))
    full = lambda shp: pl.BlockSpec(shp, lambda i, j: (0, 0))
    act = pl.BlockSpec((1, tm, d), lambda i, j: (i, j, 0))
    return pl.pallas_call(
        _fox_in_kernel,
        grid=(b, s // tm),
        in_specs=[act, full((1, d)), row, row,
                  _resident(w.shape, lambda i, j: (0, 0)),
                  full((1, LANES)), full((1, d)), full((1, d))],
        out_specs=[act, act, act, pl.BlockSpec((1, tm, LANES), lambda i, j: (i, j, 0))],
        out_shape=[jax.ShapeDtypeStruct((b, s, d), BF16)] * 3
        + [jax.ShapeDtypeStruct((b, s, LANES), F32)],
        scratch_shapes=[pltpu.VMEM((1, LANES), F32)],
        compiler_params=_params(("arbitrary", "arbitrary")),
        name="fox_in",
    )(x, g, shift, scale, w, bf, qg, kg)


def _fox_attn_kernel(q_ref, k_ref, v_ref, ck_ref, o_ref, m_ref, l_ref, acc_ref, *, tk):
    tq = q_ref.shape[1]
    i = pl.program_id(2)
    lane = lax.broadcasted_iota(jnp.int32, (1, LANES), 1)
    q2 = q_ref[0]
    zero = jnp.zeros_like(q2)
    q_heads = (jnp.where(lane < HEAD_DIM, q2, zero), jnp.where(lane >= HEAD_DIM, q2, zero))

    m_ref[...] = jnp.full_like(m_ref, -jnp.inf)
    l_ref[...] = jnp.zeros_like(l_ref)
    acc_ref[...] = jnp.zeros_like(acc_ref)

    def chunk(j, diagonal):
        start = pl.multiple_of(j * tk, tk)
        kj = k_ref[0, pl.ds(start, tk), :]
        vj = v_ref[0, pl.ds(start, tk), :]
        for hh in range(2):
            s = _dot_nt(q_heads[hh], kj) - ck_ref[0, 0, hh:hh + 1, pl.ds(start, tk)]
            if diagonal:
                r = lax.broadcasted_iota(jnp.int32, s.shape, 0)
                c = lax.broadcasted_iota(jnp.int32, s.shape, 1)
                s = jnp.where(c <= r, s, -jnp.inf)
            m_prev = m_ref[hh]
            m_new = jnp.maximum(m_prev, jnp.max(s, axis=-1, keepdims=True))
            p = jnp.exp(s - m_new)
            alpha = jnp.exp(m_prev - m_new)
            l_ref[hh] = alpha * l_ref[hh] + jnp.sum(p, axis=-1, keepdims=True)
            acc_ref[hh] = alpha * acc_ref[hh] + _dot(p.astype(BF16), vj)
            m_ref[hh] = m_new

    def body(j, carry):
        chunk(j, False)
        return carry

    lax.fori_loop(0, i, body, 0)
    chunk(i, True)

    o_a = acc_ref[0] * (1.0 / l_ref[0])
    o_b = acc_ref[1] * (1.0 / l_ref[1])
    o_ref[0] = jnp.where(lane < HEAD_DIM, o_a, o_b).astype(o_ref.dtype)


def _fox_attn(q, k, v, cum, tq=512):
    b, s, d = q.shape
    n_pairs = d // LANES
    ck = cum[:, :, :2 * n_pairs].transpose(0, 2, 1).reshape(b, n_pairs, 2, s)
    kv = pl.BlockSpec((1, s, LANES), lambda bi, p, i: (bi, 0, p))
    qo = pl.BlockSpec((1, tq, LANES), lambda bi, p, i: (bi, i, p))
    return pl.pallas_call(
        functools.partial(_fox_attn_kernel, tk=tq),
        grid=(b, n_pairs, s // tq),
        in_specs=[qo, kv, kv, pl.BlockSpec((1, 1, 2, s), lambda bi, p, i: (bi, p, 0, 0))],
        out_specs=qo,
        out_shape=jax.ShapeDtypeStruct((b, s, d), BF16),
        scratch_shapes=[pltpu.VMEM((2, tq, 1), F32), pltpu.VMEM((2, tq, 1), F32),
                        pltpu.VMEM((2, tq, LANES), F32)],
        compiler_params=_params(("arbitrary", "arbitrary", "arbitrary")),
        name="fox_attn",
    )(q, k, v, ck)


def _out_proj_kernel(x_ref, o_ref, w_ref, gt_ref, out_ref):
    out_ref[0] = x_ref[0] + gt_ref[0] * _dot(o_ref[0], w_ref[...])


def _out_proj(x, o, w_out, gate, tm=512):
    b, s, d = x.shape
    act = pl.BlockSpec((1, tm, d), lambda i, j: (i, j, 0))
    return pl.pallas_call(
        _out_proj_kernel,
        grid=(b, s // tm),
        in_specs=[act, act, _resident(w_out.shape, lambda i, j: (0, 0)),
                  pl.BlockSpec((1, 1, d), lambda i, j: (i, 0, 0))],
        out_specs=act,
        out_shape=jax.ShapeDtypeStruct(x.shape, x.dtype),
        compiler_params=_params(("arbitrary", "arbitrary")),
        name="out_proj",
    )(x, o, w_out.astype(BF16), gate)


def _rope_kernel(pos_ref, invf_ref, cos_ref, sin_ref):
    ang = pos_ref[0].astype(F32) * invf_ref[...]
    d = lax.broadcasted_iota(jnp.int32, ang.shape, 1) % HEAD_DIM
    half = ROPE_DIM // 2
    cos_ref[0] = jnp.where(d < ROPE_DIM, jnp.cos(ang), 1.0)
    sn = jnp.sin(ang)
    sin_ref[0] = jnp.where(d < half, -sn, jnp.where(d < ROPE_DIM, sn, 0.0))


def _rope_tables(positions, tm=512):
    b, s = positions.shape
    half = ROPE_DIM // 2
    inv_freq = ROPE_THETA ** (-(jnp.arange(half, dtype=F32) * 2.0 / ROPE_DIM))
    d = jnp.arange(LANES) % HEAD_DIM
    invf = jnp.where(d < ROPE_DIM, inv_freq[d % half], 0.0).reshape(1, LANES)
    tab = pl.BlockSpec((1, tm, LANES), lambda i, j: (i, j, 0))
    return pl.pallas_call(
        _rope_kernel,
        grid=(b, s // tm),
        in_specs=[pl.BlockSpec((1, tm, 1), lambda i, j: (i, j, 0)),
                  pl.BlockSpec((1, LANES), lambda i, j: (0, 0))],
        out_specs=[tab, tab],
        out_shape=[jax.ShapeDtypeStruct((b, s, LANES), F32)] * 2,
        compiler_params=_params(("arbitrary", "arbitrary")),
        name="rope_tables",
    )(positions.reshape(b, s, 1), invf)


def _rope(x, cos, sin):
    half = ROPE_DIM // 2
    d = lax.broadcasted_iota(jnp.int32, (1, LANES), 1) % HEAD_DIM
    outs = []
    for c in range(x.shape[1] // LANES):
        xc = x[:, c * LANES:(c + 1) * LANES]
        partner = jnp.where(d < half, pltpu.roll(xc, LANES - half, 1), pltpu.roll(xc, half, 1))
        outs.append(xc * cos + partner * sin)
    return jnp.concatenate(outs, axis=1)


def _dil_in_kernel(x_ref, g_ref, sh_ref, sc_ref, w_ref, qg_ref, kg_ref, cos_ref, sin_ref,
                   *out_refs):
    d = x_ref.shape[2]
    h = _norm_mod(x_ref[0], g_ref[...], sh_ref[0], sc_ref[0]).astype(BF16)
    cos = cos_ref[0]
    sin = sin_ref[0]
    for grp in range(len(DIL_CONFIGS)):
        base = grp * 3 * d
        q = _head_rms_norm(_dot(h, w_ref[:, base:base + d]), qg_ref[grp:grp + 1, :])
        out_refs[3 * grp][0] = (_rope(q, cos, sin) * (HEAD_DIM ** -0.5)).astype(BF16)
        k = _head_rms_norm(_dot(h, w_ref[:, base + d:base + 2 * d]), kg_ref[grp:grp + 1, :])
        out_refs[3 * grp + 1][0] = _rope(k, cos, sin).astype(BF16)
        out_refs[3 * grp + 2][0] = _dot(h, w_ref[:, base + 2 * d:base + 3 * d]).astype(BF16)


def _dil_in(x, g, shift, scale, w_in, q_g, k_g, cos, sin, tm=256):
    b, s, d = x.shape
    n_groups = len(DIL_CONFIGS)
    n_heads = d // HEAD_DIM
    qg = jnp.tile(q_g, (1, n_heads))
    kg = jnp.tile(k_g, (1, n_heads))
    row = pl.BlockSpec((1, 1, d), lambda i, j: (i, 0, 0))
    full = lambda shp: pl.BlockSpec(shp, lambda i, j: (0, 0))
    act = pl.BlockSpec((1, tm, d), lambda i, j: (i, j, 0))
    tab = pl.BlockSpec((1, tm, LANES), lambda i, j: (i, j, 0))
    w = w_in.astype(BF16)
    return pl.pallas_call(
        _dil_in_kernel,
        grid=(b, s // tm),
        in_specs=[act, full((1, d)), row, row, _resident(w.shape, lambda i, j: (0, 0)),
                  full((n_groups, d)), full((n_groups, d)), tab, tab],
        out_specs=[act] * (3 * n_groups),
        out_shape=[jax.ShapeDtypeStruct((b, s, d), BF16)] * (3 * n_groups),
        compiler_params=_params(("arbitrary", "arbitrary")),
        name="dil_in",
    )(x, g, shift, scale, w, qg, kg, cos, sin)


def _dil_attn_kernel(q_ref, kc_ref, kp_ref, vc_ref, vp_ref, o_ref, lse_ref):
    w = DIL_WINDOW_KEYS
    tq = q_ref.shape[1]
    d = q_ref.shape[2]
    i = pl.program_id(2)
    lane = lax.broadcasted_iota(jnp.int32, (1, LANES), 1)
    r = lax.broadcasted_iota(jnp.int32, (w, 2 * w), 0)
    c = lax.broadcasted_iota(jnp.int32, (w, 2 * w), 1)
    band = jnp.logical_and(c >= r, c <= r + w)
    bias_band = jnp.where(band, 0.0, -jnp.inf)
    in_seq = jnp.logical_or(c >= w, i > 0)
    bias_first = jnp.where(jnp.logical_and(band, in_seq), 0.0, -jnp.inf)

    lse_ref[0] = jnp.zeros((tq, LANES), F32)

    def pair(p, carry):
        cs = pl.ds(pl.multiple_of(p * LANES, LANES), LANES)
        q2 = q_ref[0, :, cs]
        kwin = jnp.concatenate([kp_ref[0, :, cs], kc_ref[0, :, cs]], axis=0)
        vwin = jnp.concatenate([vp_ref[0, :, cs], vc_ref[0, :, cs]], axis=0)
        zero = jnp.zeros((w, LANES), q2.dtype)
        for sub in range(tq // w):
            rows = slice(sub * w, (sub + 1) * w)
            qs = q2[rows]
            kw = kwin[sub * w:(sub + 2) * w]
            vw = vwin[sub * w:(sub + 2) * w]
            bias = bias_first if sub == 0 else bias_band
            outs = []
            lses = []
            for hh in range(2):
                head = (lane < HEAD_DIM) if hh == 0 else (lane >= HEAD_DIM)
                s = _dot_nt(jnp.where(head, qs, zero), kw) + bias
                m = jnp.max(s, axis=-1, keepdims=True)
                e = jnp.exp(s - m)
                l = jnp.sum(e, axis=-1, keepdims=True)
                outs.append(_dot(e.astype(BF16), vw) * (1.0 / l))
                lses.append(m + jnp.log(l))
            o_ref[0, rows, cs] = jnp.where(lane < HEAD_DIM, outs[0], outs[1])
            old = lse_ref[0, rows, :]
            lse_ref[0, rows, :] = jnp.where(lane == 2 * p, lses[0],
                                            jnp.where(lane == 2 * p + 1, lses[1], old))
        return carry

    lax.fori_loop(0, d // LANES, pair, 0)


def _dil_attn(q, k, v, dilation, tq=256):
    b, s, d = q.shape
    w = DIL_WINDOW_KEYS
    sub_len = s // dilation
    tq = min(tq, sub_len)
    view = lambda a: a.reshape(b, sub_len, dilation * d)
    cur = pl.BlockSpec((1, tq, d), lambda bi, r, i: (bi, i, r))
    prev = pl.BlockSpec((1, w, d), lambda bi, r, i: (bi, jnp.maximum(i * (tq // w) - 1, 0), r))
    o, lse = pl.pallas_call(
        _dil_attn_kernel,
        grid=(b, dilation, sub_len // tq),
        in_specs=[cur, cur, prev, cur, prev],
        out_specs=[cur, pl.BlockSpec((1, tq, LANES), lambda bi, r, i: (bi, i, r))],
        out_shape=[jax.ShapeDtypeStruct((b, sub_len, dilation * d), F32),
                   jax.ShapeDtypeStruct((b, sub_len, dilation * LANES), F32)],
        compiler_params=_params(("arbitrary", "arbitrary", "arbitrary")),
        name="dil_attn",
    )(view(q), view(k), view(k), view(v), view(v))
    return o.reshape(b, s, d), lse.reshape(b, s, LANES)


def _dil_out_kernel(x_ref, o0_ref, o1_ref, o2_ref, l0_ref, l1_ref, l2_ref, w_ref, gt_ref,
                    out_ref):
    d = x_ref.shape[2]
    lses = [l0_ref[0], l1_ref[0], l2_ref[0]]
    m = jnp.maximum(jnp.maximum(lses[0], lses[1]), lses[2])
    es = [jnp.exp(l - m) for l in lses]
    inv = 1.0 / (es[0] + es[1] + es[2])
    r = lax.broadcasted_iota(jnp.int32, (LANES, d), 0)
    c = lax.broadcasted_iota(jnp.int32, (LANES, d), 1) // HEAD_DIM
    expand = jnp.where(r == c, 1.0, 0.0).astype(BF16)
    merged = jnp.zeros(x_ref.shape[1:], F32)
    for e, o_ref in zip(es, (o0_ref, o1_ref, o2_ref)):
        alpha = e * inv
        hi = alpha.astype(BF16)
        lo = (alpha - hi.astype(F32)).astype(BF16)
        merged = merged + (_dot(hi, expand) + _dot(lo, expand)) * o_ref[0]
    out_ref[0] = x_ref[0] + gt_ref[0] * _dot(merged.astype(BF16), w_ref[...])


def _dil_out(x, outs, lses, w_out, gate, tm=512):
    b, s, d = x.shape
    act = pl.BlockSpec((1, tm, d), lambda i, j: (i, j, 0))
    lse = pl.BlockSpec((1, tm, LANES), lambda i, j: (i, j, 0))
    return pl.pallas_call(
        _dil_out_kernel,
        grid=(b, s // tm),
        in_specs=[act, act, act, act, lse, lse, lse,
                  _resident(w_out.shape, lambda i, j: (0, 0)),
                  pl.BlockSpec((1, 1, d), lambda i, j: (i, 0, 0))],
        out_specs=act,
        out_shape=jax.ShapeDtypeStruct(x.shape, x.dtype),
        compiler_params=_params(("arbitrary", "arbitrary")),
        name="dil_out",
    )(x, *outs, *lses, w_out.astype(BF16), gate)


def kernel(x, c, positions, mod_w, mod_b, norm_g, ffn_w_gate, ffn_w_up, ffn_w_down,
           fox_w_in, fox_b_f, fox_q_g, fox_k_g, fox_w_out,
           dil_w_in, dil_q_g, dil_k_g, dil_w_out):
    b, s, d = x.shape
    depth = mod_w.shape[0]
    mod = _modulation(c, mod_w, mod_b).reshape(depth, b, N_SUBLAYERS, 3, 1, d)
    wg = ffn_w_gate.astype(BF16)
    wu = ffn_w_up.astype(BF16)
    wd = ffn_w_down.astype(BF16)
    cos, sin = _rope_tables(positions)

    for i in range(depth):
        shift = lambda sub: mod[i, :, sub, 0]
        scale = lambda sub: mod[i, :, sub, 1]
        gate = lambda sub: mod[i, :, sub, 2]
        g = lambda sub: norm_g[i, sub].reshape(1, d)

        x = _ffn(x, g(0), shift(0), scale(0), gate(0), wg, wu, wd, i, 0)
        j = i // 2
        if i % 2 == 0:
            q, k, v, cum = _fox_in(x, g(1), shift(1), scale(1), fox_w_in[j], fox_b_f[j],
                                   fox_q_g[j], fox_k_g[j])
            o = _fox_attn(q, k, v, cum)
            x = _out_proj(x, o, fox_w_out[j], gate(1))
        else:
            qkv = _dil_in(x, g(1), shift(1), scale(1), dil_w_in[j], dil_q_g[j], dil_k_g[j],
                          cos, sin)
            outs, lses = [], []
            for grp, (_, dilation) in enumerate(DIL_CONFIGS):
                o, lse = _dil_attn(qkv[3 * grp], qkv[3 * grp + 1], qkv[3 * grp + 2], dilation)
                outs.append(o)
                lses.append(lse)
            x = _dil_out(x, outs, lses, dil_w_out[j], gate(1))
        x = _ffn(x, g(2), shift(2), scale(2), gate(2), wg, wu, wd, i, 1)
    return x
```

```python
import functools

import jax
import jax.numpy as jnp
from jax import lax
from jax.experimental import pallas as pl
from jax.experimental.pallas import tpu as pltpu

HEAD_DIM = 64
ROPE_DIM = 16
ROPE_THETA = 500000.0
EPS = 1e-6
MACARON_WEIGHT = 0.5
N_SUBLAYERS = 3
DIL_CONFIGS = ((128, 1), (512, 4), (2048, 16))
DIL_WINDOW_KEYS = 128

LANES = 128
MXU_DIM = 256
VMEM_LIMIT = 56 * 1024 * 1024

F32 = jnp.float32
BF16 = jnp.bfloat16


def _params(semantics, vmem=VMEM_LIMIT):
    return pltpu.CompilerParams(dimension_semantics=semantics, vmem_limit_bytes=vmem)


def _resident(block_shape, index_map):
    return pl.BlockSpec(block_shape, index_map, pipeline_mode=pl.Buffered(1))


def _dot(a, b):
    return jnp.dot(a, b, preferred_element_type=F32)


def _dot_nt(a, b):
    return lax.dot_general(a, b, (((1,), (1,)), ((), ())), preferred_element_type=F32)


def _sigmoid(x):
    return 1.0 / (1.0 + jnp.exp(-x))


def _norm_mod(x, g, shift, scale):
    ms = jnp.mean(x * x, axis=-1, keepdims=True)
    y = x * lax.rsqrt(ms + EPS) * g
    return y * (1.0 + scale) + shift


def _head_block_ones(n):
    r = lax.broadcasted_iota(jnp.int32, (n, n), 0) // HEAD_DIM
    c = lax.broadcasted_iota(jnp.int32, (n, n), 1) // HEAD_DIM
    return jnp.where(r == c, 1.0, 0.0).astype(BF16)


def _head_rms_norm(x, gain):
    ones = _head_block_ones(MXU_DIM)
    outs = []
    for c in range(x.shape[1] // MXU_DIM):
        xc = x[:, c * MXU_DIM:(c + 1) * MXU_DIM]
        ssq = _dot((xc * xc).astype(BF16), ones)
        r = lax.rsqrt(ssq * (1.0 / HEAD_DIM) + EPS)
        outs.append(xc * r * gain[:, c * MXU_DIM:(c + 1) * MXU_DIM])
    return jnp.concatenate(outs, axis=1)


def _mod_kernel(c_ref, w_ref, b_ref, o_ref):
    c = c_ref[...]
    c_act = c * _sigmoid(c)
    o_ref[0] = jnp.dot(c_act, w_ref[0], preferred_element_type=F32,
                       precision=lax.Precision.HIGHEST) + b_ref[0]


def _modulation(c, mod_w, mod_b):
    depth, d, n = mod_w.shape
    b = c.shape[0]
    tn = 1024
    return pl.pallas_call(
        _mod_kernel,
        grid=(depth, n // tn),
        in_specs=[
            pl.BlockSpec((b, d), lambda i, j: (0, 0)),
            pl.BlockSpec((1, d, tn), lambda i, j: (i, 0, j)),
            pl.BlockSpec((1, 1, tn), lambda i, j: (i, 0, j)),
        ],
        out_specs=pl.BlockSpec((1, b, tn), lambda i, j: (i, 0, j)),
        out_shape=jax.ShapeDtypeStruct((depth, b, n), F32),
        compiler_params=_params(("arbitrary", "arbitrary")),
        name="modulation",
    )(c, mod_w, mod_b.reshape(depth, 1, n))


FFN_CHUNK = 256


def _ffn_kernel(x_ref, g_ref, sh_ref, sc_ref, gt_ref, wg_ref, wu_ref, wd_ref, o_ref):
    x = x_ref[0]
    h = _norm_mod(x, g_ref[...], sh_ref[0], sc_ref[0]).astype(BF16)
    acc = jnp.zeros_like(x)
    for c in range(wg_ref.shape[1] // FFN_CHUNK):
        cs = slice(c * FFN_CHUNK, (c + 1) * FFN_CHUNK)
        gg = _dot(h, wg_ref[:, cs])
        uu = _dot(h, wu_ref[:, cs])
        a = (gg * _sigmoid(gg) * uu).astype(BF16)
        acc = acc + _dot(a, wd_ref[cs, :])
    o_ref[0] = x + (MACARON_WEIGHT * gt_ref[0]) * acc


def _ffn(x, g, shift, scale, gate, wg, wu, wd, layer, which, tm=512):
    b, s, d = x.shape
    f = wg.shape[-1]
    row = pl.BlockSpec((1, 1, d), lambda i, j: (i, 0, 0))
    return pl.pallas_call(
        _ffn_kernel,
        grid=(b, s // tm),
        in_specs=[
            pl.BlockSpec((1, tm, d), lambda i, j: (i, j, 0)),
            pl.BlockSpec((1, d), lambda i, j: (0, 0)),
            row, row, row,
            _resident((None, None, d, f), lambda i, j: (layer, which, 0, 0)),
            _resident((None, None, d, f), lambda i, j: (layer, which, 0, 0)),
            _resident((None, None, f, d), lambda i, j: (layer, which, 0, 0)),
        ],
        out_specs=pl.BlockSpec((1, tm, d), lambda i, j: (i, j, 0)),
        out_shape=jax.ShapeDtypeStruct(x.shape, x.dtype),
        compiler_params=_params(("arbitrary", "arbitrary")),
        name="ffn",
    )(x, g, shift, scale, gate, wg, wu, wd)


def _cumsum_rows(x):
    n = x.shape[0]
    row = lax.broadcasted_iota(jnp.int32, x.shape, 0)
    shift = 1
    while shift < n:
        x = x + jnp.where(row >= shift, pltpu.roll(x, shift, 0), 0.0)
        shift *= 2
    return x


N_SPLIT = 3
LOG2E = 1.4426950408889634


def _fox_in_kernel(x_ref, g_ref, sh_ref, sc_ref, wqk_ref, wvt_ref, wf_ref, bf_ref, qg_ref,
                   kg_ref, sel_ref, q_ref, ka_ref, vt_ref, carry_ref, *, n_heads):
    d = x_ref.shape[2]
    tm = x_ref.shape[1]

    @pl.when(pl.program_id(1) == 0)
    def _():
        carry_ref[...] = jnp.zeros_like(carry_ref)

    h = _norm_mod(x_ref[0], g_ref[...], sh_ref[0], sc_ref[0]).astype(BF16)
    q = _head_rms_norm(_dot(h, wqk_ref[:, 0:d]), qg_ref[...])
    q_ref[0] = (q * (LOG2E * HEAD_DIM ** -0.5)).astype(BF16)
    k = _head_rms_norm(_dot(h, wqk_ref[:, d:2 * d]), kg_ref[...]).astype(BF16)
    vt_ref[0] = _dot_nt(wvt_ref[...], h).astype(BF16)

    z = _dot(h, wf_ref[...]) + bf_ref[...]
    logf = jnp.minimum(z, 0.0) - jnp.log1p(jnp.exp(-jnp.abs(z)))
    lane = lax.broadcasted_iota(jnp.int32, logf.shape, 1)
    logf = jnp.where(lane < n_heads, logf, 0.0)
    cum = _cumsum_rows(logf) + carry_ref[...]
    carry_ref[...] = cum[tm - 1:tm, :]

    terms = []
    rest = cum * LOG2E
    for _ in range(N_SPLIT):
        t = rest.astype(BF16)
        terms.append(t)
        rest = rest - t.astype(F32)
    kx = _dot(jnp.concatenate(terms, axis=1), sel_ref[...]).astype(BF16)
    pieces = []
    for p in range(d // LANES):
        pieces += [k[:, p * LANES:(p + 1) * LANES], kx[:, p * LANES:(p + 1) * LANES]]
    ka_ref[0] = jnp.concatenate(pieces, axis=1)


def _fox_in(x, g, shift, scale, w_in, b_f, q_g, k_g, tm=512):
    b, s, d = x.shape
    n_heads = d // HEAD_DIM
    wqk = w_in[:, :2 * d].astype(BF16)
    wvt = w_in[:, 2 * d:3 * d].T.astype(BF16)
    wf = jnp.pad(w_in[:, 3 * d:], ((0, 0), (0, LANES - n_heads))).astype(BF16)
    bf = jnp.pad(b_f, (0, LANES - n_heads)).reshape(1, LANES)
    qg = jnp.tile(q_g, n_heads).reshape(1, d)
    kg = jnp.tile(k_g, n_heads).reshape(1, d)
    heads = jnp.arange(n_heads)
    sel = jnp.zeros((N_SPLIT * LANES, d), BF16)
    for t in range(N_SPLIT):
        sel = sel.at[t * LANES + heads, (heads // 2) * LANES + (heads % 2) * N_SPLIT + t].set(1.0)
    row = pl.BlockSpec((1, 1, d), lambda i, j: (i, 0, 0))
    full = lambda shp: pl.BlockSpec(shp, lambda i, j: (0, 0))
    res = lambda a: _resident(a.shape, lambda i, j: (0, 0))
    act = pl.BlockSpec((1, tm, d), lambda i, j: (i, j, 0))
    return pl.pallas_call(
        functools.partial(_fox_in_kernel, n_heads=n_heads),
        grid=(b, s // tm),
        in_specs=[act, full((1, d)), row, row, res(wqk), res(wvt), res(wf),
                  full((1, LANES)), full((1, d)), full((1, d)), res(sel)],
        out_specs=[act, pl.BlockSpec((1, tm, 2 * d), lambda i, j: (i, j, 0)),
                   pl.BlockSpec((1, d, tm), lambda i, j: (i, 0, j))],
        out_shape=[jax.ShapeDtypeStruct((b, s, d), BF16),
                   jax.ShapeDtypeStruct((b, s, 2 * d), BF16),
                   jax.ShapeDtypeStruct((b, d, s), BF16)],
        scratch_shapes=[pltpu.VMEM((1, LANES), F32)],
        compiler_params=_params(("arbitrary", "arbitrary")),
        name="fox_in",
    )(x, g, shift, scale, wqk, wvt, wf, bf, qg, kg, sel)


ONES_ROWS = 16


def _fox_attn_kernel(q_ref, ka_ref, vt_ref, o_ref, m_ref, acc_ref, s_ref, *, tk, strip):
    tq = q_ref.shape[1]
    i = pl.program_id(2)
    lane = lax.broadcasted_iota(jnp.int32, (1, LANES), 1)
    q2 = q_ref[0]
    zero = jnp.zeros_like(q2)
    q_heads = []
    for hh in range(2):
        head = (lane < HEAD_DIM) if hh == 0 else (lane >= HEAD_DIM)
        pick = jnp.logical_and(lane >= hh * N_SPLIT, lane < (hh + 1) * N_SPLIT)
        minus_one = jnp.broadcast_to(jnp.where(pick, -1.0, 0.0), q2.shape).astype(BF16)
        q_heads.append(jnp.concatenate([jnp.where(head, q2, zero), minus_one], axis=1))

    m_ref[...] = jnp.full_like(m_ref, -jnp.inf)
    acc_ref[...] = jnp.zeros_like(acc_ref)
    ones_rows = jnp.ones((ONES_ROWS, tk), BF16)

    chains = [(hh, slice(st * strip, (st + 1) * strip), st)
              for hh in range(2) for st in range(tq // strip)]

    def logits(c, j, masked):
        hh, cols, st = chains[c]
        start = pl.multiple_of(j * tk, tk)
        s = _dot_nt(ka_ref[0, pl.ds(start, tk), :], q_heads[hh][cols])
        if masked:
            key = lax.broadcasted_iota(jnp.int32, s.shape, 0) + j * tk
            qry = lax.broadcasted_iota(jnp.int32, s.shape, 1) + (i * tq + st * strip)
            s = jnp.where(key <= qry, s, -jnp.inf)
        return s

    def softmax_pv(c, j):
        hh, cols, _ = chains[c]
        start = pl.multiple_of(j * tk, tk)
        s = s_ref[c]
        m_prev = m_ref[hh, :, cols]
        m_new = jnp.maximum(m_prev, jnp.max(s, axis=0, keepdims=True))
        p = jnp.exp2(s - m_new)
        alpha = jnp.exp2(m_prev - m_new)
        m_ref[hh, :, cols] = m_new
        vt_ones = jnp.concatenate([vt_ref[0, :, pl.ds(start, tk)], ones_rows], axis=0)
        acc_ref[hh, :, cols] = alpha * acc_ref[hh, :, cols] + _dot(vt_ones, p.astype(BF16))

    def step(j, masked_next):
        n = len(chains)
        ahead = 2
        s_next = [logits(c, j + 1, masked_next) for c in range(ahead)]
        for c in range(n):
            if c + ahead < n:
                s_next.append(logits(c + ahead, j + 1, masked_next))
            softmax_pv(c, j)
            s_ref[c] = s_next[c]

    for c in range(len(chains)):
        s_ref[c] = logits(c, 0, True)

    def body(j, carry):
        step(j, False)
        return carry

    lax.fori_loop(0, i - 1, body, 0)

    @pl.when(i > 0)
    def _():
        step(i - 1, True)

    for c in range(len(chains)):
        softmax_pv(c, i)

    row = lax.broadcasted_iota(jnp.int32, (LANES, 1), 0)
    o_heads = [acc_ref[hh, :LANES, :] * (1.0 / acc_ref[hh, LANES:LANES + 1, :]) for hh in range(2)]
    o_ref[0] = jnp.where(row < HEAD_DIM, o_heads[0], o_heads[1]).T.astype(o_ref.dtype)


def _fox_attn(q, ka, vt, tq=512, tk=512):
    b, s, d = q.shape
    qo = pl.BlockSpec((1, tq, LANES), lambda bi, p, i: (bi, i, p))
    return pl.pallas_call(
        functools.partial(_fox_attn_kernel, tk=tk, strip=MXU_DIM),
        grid=(b, d // LANES, s // tq),
        in_specs=[qo, pl.BlockSpec((1, s, 2 * LANES), lambda bi, p, i: (bi, 0, p)),
                  pl.BlockSpec((1, LANES, s), lambda bi, p, i: (bi, p, 0))],
        out_specs=qo,
        out_shape=jax.ShapeDtypeStruct((b, s, d), BF16),
        scratch_shapes=[pltpu.VMEM((2, 1, tq), F32),
                        pltpu.VMEM((2, LANES + ONES_ROWS, tq), F32),
                        pltpu.VMEM((2 * tq // MXU_DIM, tk, MXU_DIM), F32)],
        compiler_params=_params(("arbitrary", "arbitrary", "arbitrary")),
        name="fox_attn",
    )(q, ka, vt)


def _out_proj_kernel(x_ref, o_ref, w_ref, gt_ref, out_ref):
    out_ref[0] = x_ref[0] + gt_ref[0] * _dot(o_ref[0], w_ref[...])


def _out_proj(x, o, w_out, gate, tm=512):
    b, s, d = x.shape
    act = pl.BlockSpec((1, tm, d), lambda i, j: (i, j, 0))
    return pl.pallas_call(
        _out_proj_kernel,
        grid=(b, s // tm),
        in_specs=[act, act, _resident(w_out.shape, lambda i, j: (0, 0)),
                  pl.BlockSpec((1, 1, d), lambda i, j: (i, 0, 0))],
        out_specs=act,
        out_shape=jax.ShapeDtypeStruct(x.shape, x.dtype),
        compiler_params=_params(("arbitrary", "arbitrary")),
        name="out_proj",
    )(x, o, w_out.astype(BF16), gate)


def _rope_kernel(pos_ref, invf_ref, cos_ref, sin_ref):
    ang = pos_ref[0].astype(F32) * invf_ref[...]
    d = lax.broadcasted_iota(jnp.int32, ang.shape, 1) % HEAD_DIM
    half = ROPE_DIM // 2
    cos_ref[0] = jnp.where(d < ROPE_DIM, jnp.cos(ang), 1.0)
    sn = jnp.sin(ang)
    sin_ref[0] = jnp.where(d < half, -sn, jnp.where(d < ROPE_DIM, sn, 0.0))


def _rope_tables(positions, tm=512):
    b, s = positions.shape
    half = ROPE_DIM // 2
    inv_freq = ROPE_THETA ** (-(jnp.arange(half, dtype=F32) * 2.0 / ROPE_DIM))
    d = jnp.arange(LANES) % HEAD_DIM
    invf = jnp.where(d < ROPE_DIM, inv_freq[d % half], 0.0).reshape(1, LANES)
    tab = pl.BlockSpec((1, tm, LANES), lambda i, j: (i, j, 0))
    return pl.pallas_call(
        _rope_kernel,
        grid=(b, s // tm),
        in_specs=[pl.BlockSpec((1, tm, 1), lambda i, j: (i, j, 0)),
                  pl.BlockSpec((1, LANES), lambda i, j: (0, 0))],
        out_specs=[tab, tab],
        out_shape=[jax.ShapeDtypeStruct((b, s, LANES), F32)] * 2,
        compiler_params=_params(("arbitrary", "arbitrary")),
        name="rope_tables",
    )(positions.reshape(b, s, 1), invf)


def _rope(x, cos, sin):
    half = ROPE_DIM // 2
    d = lax.broadcasted_iota(jnp.int32, (1, LANES), 1) % HEAD_DIM
    outs = []
    for c in range(x.shape[1] // LANES):
        xc = x[:, c * LANES:(c + 1) * LANES]
        partner = jnp.where(d < half, pltpu.roll(xc, LANES - half, 1), pltpu.roll(xc, half, 1))
        outs.append(xc * cos + partner * sin)
    return jnp.concatenate(outs, axis=1)


def _dil_in_kernel(x_ref, g_ref, sh_ref, sc_ref, w_ref, qg_ref, kg_ref, cos_ref, sin_ref,
                   *refs):
    out_refs, stage_ref = refs[:-1], refs[-1]
    tm, d = x_ref.shape[1:]
    h = _norm_mod(x_ref[0], g_ref[...], sh_ref[0], sc_ref[0]).astype(BF16)
    cos = cos_ref[0]
    sin = sin_ref[0]
    n_staged = 0
    for grp, (_, dilation) in enumerate(DIL_CONFIGS):
        base = grp * 3 * d
        q = _head_rms_norm(_dot(h, w_ref[:, base:base + d]), qg_ref[grp:grp + 1, :])
        k = _head_rms_norm(_dot(h, w_ref[:, base + d:base + 2 * d]), kg_ref[grp:grp + 1, :])
        vals = (_rope(q, cos, sin) * (LOG2E * HEAD_DIM ** -0.5), _rope(k, cos, sin),
                _dot(h, w_ref[:, base + 2 * d:base + 3 * d]))
        for out_ref, val in zip(out_refs[3 * grp:3 * grp + 3], vals):
            if dilation == 1:
                out_ref[0, 0] = val.astype(BF16)
                continue
            n_tiles = d // LANES
            for t in range(n_tiles):
                stage_ref[n_staged * n_tiles + t] = val[:, t * LANES:(t + 1) * LANES]
            for r in range(dilation):
                rows = pl.ds(r, tm // dilation, stride=dilation)
                out_ref[0, r] = jnp.concatenate(
                    [stage_ref[n_staged * n_tiles + t, rows, :] for t in range(n_tiles)],
                    axis=1).astype(BF16)
            n_staged += 1


def _dil_in(x, g, shift, scale, w_in, q_g, k_g, cos, sin, tm=256):
    b, s, d = x.shape
    n_groups = len(DIL_CONFIGS)
    n_heads = d // HEAD_DIM
    qg = jnp.tile(q_g, (1, n_heads))
    kg = jnp.tile(k_g, (1, n_heads))
    row = pl.BlockSpec((1, 1, d), lambda i, j: (i, 0, 0))
    full = lambda shp: pl.BlockSpec(shp, lambda i, j: (0, 0))
    act = pl.BlockSpec((1, tm, d), lambda i, j: (i, j, 0))
    tab = pl.BlockSpec((1, tm, LANES), lambda i, j: (i, j, 0))
    w = w_in.astype(BF16)
    out_specs, out_shape = [], []
    for _, dil in DIL_CONFIGS:
        out_specs += [pl.BlockSpec((1, dil, tm // dil, d), lambda i, j: (i, 0, j, 0))] * 3
        out_shape += [jax.ShapeDtypeStruct((b, dil, s // dil, d), BF16)] * 3
    n_staged = 3 * sum(1 for _, dil in DIL_CONFIGS if dil > 1)
    return pl.pallas_call(
        _dil_in_kernel,
        grid=(b, s // tm),
        in_specs=[act, full((1, d)), row, row, _resident(w.shape, lambda i, j: (0, 0)),
                  full((n_groups, d)), full((n_groups, d)), tab, tab],
        out_specs=out_specs,
        out_shape=out_shape,
        scratch_shapes=[pltpu.VMEM((n_staged * d // LANES, tm, LANES), F32)],
        compiler_params=_params(("arbitrary", "arbitrary")),
        name="dil_in",
    )(x, g, shift, scale, w, qg, kg, cos, sin)


def _dil_attn_kernel(q_ref, kc_ref, kp_ref, vc_ref, vp_ref, o_ref, lse_ref, lse_t_ref):
    w = DIL_WINDOW_KEYS
    tq, d = q_ref.shape[2:]
    n_heads = d // HEAD_DIM
    i = pl.program_id(2)
    lane = lax.broadcasted_iota(jnp.int32, (1, LANES), 1)
    key = lax.broadcasted_iota(jnp.int32, (2 * w, 2 * w), 0)
    qry = lax.broadcasted_iota(jnp.int32, (2 * w, 2 * w), 1) % w
    band = jnp.logical_and(key >= qry, key <= qry + w)
    bias_band = jnp.where(band, 0.0, -jnp.inf)
    in_seq = jnp.logical_or(key >= w, i > 0)
    bias_first = jnp.where(jnp.logical_and(band, in_seq), 0.0, -jnp.inf)
    dim = lax.broadcasted_iota(jnp.int32, (LANES, 1), 0)

    chains = [(p, sub) for p in range(d // LANES) for sub in range(tq // w)]
    zero = jnp.zeros((w, LANES), q_ref.dtype)

    def window(cur_ref, prev_ref, p, sub):
        cs = slice(p * LANES, (p + 1) * LANES)
        if sub == 0:
            return jnp.concatenate([prev_ref[0, 0, :, cs], cur_ref[0, 0, :w, cs]], axis=0)
        return cur_ref[0, 0, (sub - 1) * w:(sub + 1) * w, cs]

    def logits(c):
        p, sub = chains[c]
        qs = q_ref[0, 0, sub * w:(sub + 1) * w, p * LANES:(p + 1) * LANES]
        q_both = jnp.concatenate([jnp.where(lane < HEAD_DIM, qs, zero),
                                  jnp.where(lane >= HEAD_DIM, qs, zero)], axis=0)
        s = _dot_nt(window(kc_ref, kp_ref, p, sub), q_both)
        return s + (bias_first if sub == 0 else bias_band)

    def finish(c, s):
        p, sub = chains[c]
        rows = slice(sub * w, (sub + 1) * w)
        m = jnp.max(s, axis=0, keepdims=True)
        e = jnp.exp2(s - m)
        l = jnp.sum(e, axis=0, keepdims=True)
        o_t = lax.dot_general(window(vc_ref, vp_ref, p, sub), e.astype(BF16),
                              (((0,), (0,)), ((), ())),
                              preferred_element_type=F32) * (1.0 / l)
        o_ref[0, 0, rows, p * LANES:(p + 1) * LANES] = (
            jnp.where(dim < HEAD_DIM, o_t[:, :w], o_t[:, w:]).T)
        lse = m + jnp.log(l) * LOG2E
        lse_t_ref[2 * p:2 * p + 1, rows] = lse[:, :w]
        lse_t_ref[2 * p + 1:2 * p + 2, rows] = lse[:, w:]

    lse_t_ref[...] = jnp.zeros_like(lse_t_ref)
    ahead = 2
    pending = [logits(c) for c in range(ahead)]
    for c in range(len(chains)):
        if c + ahead < len(chains):
            pending.append(logits(c + ahead))
        finish(c, pending[c])
    lse_ref[0, 0] = lse_t_ref[...].T


def _dil_attn(q, k, v, tq=256):
    b, dilation, sub_len, d = q.shape
    w = DIL_WINDOW_KEYS
    tq = min(tq, sub_len)
    cur = pl.BlockSpec((1, 1, tq, d), lambda bi, r, i: (bi, r, i, 0))
    prev = pl.BlockSpec((1, 1, w, d),
                        lambda bi, r, i: (bi, r, jnp.maximum(i * (tq // w) - 1, 0), 0))
    return pl.pallas_call(
        _dil_attn_kernel,
        grid=(b, dilation, sub_len // tq),
        in_specs=[cur, cur, prev, cur, prev],
        out_specs=[cur, pl.BlockSpec((1, 1, tq, LANES), lambda bi, r, i: (bi, r, i, 0))],
        out_shape=[jax.ShapeDtypeStruct(q.shape, F32),
                   jax.ShapeDtypeStruct((b, dilation, sub_len, LANES), F32)],
        scratch_shapes=[pltpu.VMEM((LANES, tq), F32)],
        compiler_params=_params(("arbitrary", "arbitrary", "arbitrary")),
        name="dil_attn",
    )(q, k, k, v, v)


def _dil_out_kernel(x_ref, o0_ref, o1_ref, o2_ref, l0_ref, l1_ref, l2_ref, w_ref, gt_ref,
                    out_ref, o_stage_ref, l_stage_ref):
    d = x_ref.shape[2]

    def token_order(ref, stage_ref, slot):
        dilation, n, width = ref.shape[1:]
        if dilation == 1:
            return ref[0, 0]
        n_tiles = width // LANES
        for r in range(dilation):
            for t in range(n_tiles):
                stage_ref[slot * n_tiles + t, pl.ds(r, n, stride=dilation), :] = (
                    ref[0, r, :, t * LANES:(t + 1) * LANES])
        return jnp.concatenate([stage_ref[slot * n_tiles + t] for t in range(n_tiles)], axis=1)

    lses = [token_order(ref, l_stage_ref, g) for g, ref in enumerate((l0_ref, l1_ref, l2_ref))]
    outs = [token_order(ref, o_stage_ref, g) for g, ref in enumerate((o0_ref, o1_ref, o2_ref))]
    m = jnp.maximum(jnp.maximum(lses[0], lses[1]), lses[2])
    es = [jnp.exp2(l - m) for l in lses]
    inv = 1.0 / (es[0] + es[1] + es[2])
    r = lax.broadcasted_iota(jnp.int32, (LANES, d), 0)
    c = lax.broadcasted_iota(jnp.int32, (LANES, d), 1) // HEAD_DIM
    expand = jnp.where(r == c, 1.0, 0.0).astype(BF16)
    merged = jnp.zeros(x_ref.shape[1:], F32)
    for e, o in zip(es, outs):
        alpha = e * inv
        hi = alpha.astype(BF16)
        lo = (alpha - hi.astype(F32)).astype(BF16)
        merged = merged + (_dot(hi, expand) + _dot(lo, expand)) * o
    out_ref[0] = x_ref[0] + gt_ref[0] * _dot(merged.astype(BF16), w_ref[...])


def _dil_out(x, outs, lses, w_out, gate, tm=512):
    b, s, d = x.shape
    act = pl.BlockSpec((1, tm, d), lambda i, j: (i, j, 0))
    grouped = lambda a: pl.BlockSpec((1, a.shape[1], tm // a.shape[1], a.shape[3]),
                                     lambda i, j: (i, 0, j, 0))
    return pl.pallas_call(
        _dil_out_kernel,
        grid=(b, s // tm),
        in_specs=[act] + [grouped(a) for a in outs] + [grouped(a) for a in lses]
        + [_resident(w_out.shape, lambda i, j: (0, 0)),
           pl.BlockSpec((1, 1, d), lambda i, j: (i, 0, 0))],
        out_specs=act,
        out_shape=jax.ShapeDtypeStruct(x.shape, x.dtype),
        scratch_shapes=[pltpu.VMEM((len(outs) * d // LANES, tm, LANES), F32),
                        pltpu.VMEM((len(lses), tm, LANES), F32)],
        compiler_params=_params(("arbitrary", "arbitrary")),
        name="dil_out",
    )(x, *outs, *lses, w_out.astype(BF16), gate)


def kernel(x, c, positions, mod_w, mod_b, norm_g, ffn_w_gate, ffn_w_up, ffn_w_down,
           fox_w_in, fox_b_f, fox_q_g, fox_k_g, fox_w_out,
           dil_w_in, dil_q_g, dil_k_g, dil_w_out):
    b, s, d = x.shape
    depth = mod_w.shape[0]
    mod = _modulation(c, mod_w, mod_b).reshape(depth, b, N_SUBLAYERS, 3, 1, d)
    wg = ffn_w_gate.astype(BF16)
    wu = ffn_w_up.astype(BF16)
    wd = ffn_w_down.astype(BF16)
    cos, sin = _rope_tables(positions)

    for i in range(depth):
        shift = lambda sub: mod[i, :, sub, 0]
        scale = lambda sub: mod[i, :, sub, 1]
        gate = lambda sub: mod[i, :, sub, 2]
        g = lambda sub: norm_g[i, sub].reshape(1, d)

        x = _ffn(x, g(0), shift(0), scale(0), gate(0), wg, wu, wd, i, 0)
        j = i // 2
        if i % 2 == 0:
            q, ka, vt = _fox_in(x, g(1), shift(1), scale(1), fox_w_in[j], fox_b_f[j],
                                fox_q_g[j], fox_k_g[j])
            o = _fox_attn(q, ka, vt)
            x = _out_proj(x, o, fox_w_out[j], gate(1))
        else:
            qkv = _dil_in(x, g(1), shift(1), scale(1), dil_w_in[j], dil_q_g[j], dil_k_g[j],
                          cos, sin)
            outs, lses = [], []
            for grp in range(len(DIL_CONFIGS)):
                o, lse = _dil_attn(qkv[3 * grp], qkv[3 * grp + 1], qkv[3 * grp + 2])
                outs.append(o)
                lses.append(lse)
            x = _dil_out(x, outs, lses, dil_w_out[j], gate(1))
        x = _ffn(x, g(2), shift(2), scale(2), gate(2), wg, wu, wd, i, 1)
    return x
```

```python
import functools

import jax
import jax.numpy as jnp
from jax import lax
from jax.experimental import pallas as pl
from jax.experimental.pallas import tpu as pltpu

HEAD_DIM = 64
ROPE_DIM = 16
ROPE_THETA = 500000.0
EPS = 1e-6
MACARON_WEIGHT = 0.5
N_SUBLAYERS = 3
DIL_CONFIGS = ((128, 1), (512, 4), (2048, 16))
DIL_WINDOW_KEYS = 128

LANES = 128
MXU_DIM = 256
VMEM_LIMIT = 56 * 1024 * 1024

F32 = jnp.float32
BF16 = jnp.bfloat16


def _params(semantics, vmem=VMEM_LIMIT):
    return pltpu.CompilerParams(dimension_semantics=semantics, vmem_limit_bytes=vmem)


def _resident(block_shape, index_map):
    return pl.BlockSpec(block_shape, index_map, pipeline_mode=pl.Buffered(1))


def _dot(a, b):
    return jnp.dot(a, b, preferred_element_type=F32)


def _dot_nt(a, b):
    return lax.dot_general(a, b, (((1,), (1,)), ((), ())), preferred_element_type=F32)


def _sigmoid(x):
    return 1.0 / (1.0 + jnp.exp(-x))


def _norm_mod(x, g, shift, scale):
    ms = jnp.mean(x * x, axis=-1, keepdims=True)
    y = x * lax.rsqrt(ms + EPS) * g
    return y * (1.0 + scale) + shift


def _head_block_ones(n):
    r = lax.broadcasted_iota(jnp.int32, (n, n), 0) // HEAD_DIM
    c = lax.broadcasted_iota(jnp.int32, (n, n), 1) // HEAD_DIM
    return jnp.where(r == c, 1.0, 0.0).astype(BF16)


def _head_rms_norm(x, gain):
    ones = _head_block_ones(MXU_DIM)
    outs = []
    for c in range(x.shape[1] // MXU_DIM):
        xc = x[:, c * MXU_DIM:(c + 1) * MXU_DIM]
        ssq = _dot((xc * xc).astype(BF16), ones)
        r = lax.rsqrt(ssq * (1.0 / HEAD_DIM) + EPS)
        outs.append(xc * r * gain[:, c * MXU_DIM:(c + 1) * MXU_DIM])
    return jnp.concatenate(outs, axis=1)


def _mod_kernel(c_ref, w_ref, b_ref, o_ref):
    c = c_ref[...]
    c_act = c * _sigmoid(c)
    o_ref[0] = jnp.dot(c_act, w_ref[0], preferred_element_type=F32,
                       precision=lax.Precision.HIGHEST) + b_ref[0]


def _modulation(c, mod_w, mod_b):
    depth, d, n = mod_w.shape
    b = c.shape[0]
    tn = 1024
    return pl.pallas_call(
        _mod_kernel,
        grid=(depth, n // tn),
        in_specs=[
            pl.BlockSpec((b, d), lambda i, j: (0, 0)),
            pl.BlockSpec((1, d, tn), lambda i, j: (i, 0, j)),
            pl.BlockSpec((1, 1, tn), lambda i, j: (i, 0, j)),
        ],
        out_specs=pl.BlockSpec((1, b, tn), lambda i, j: (i, 0, j)),
        out_shape=jax.ShapeDtypeStruct((depth, b, n), F32),
        compiler_params=_params(("arbitrary", "arbitrary")),
        name="modulation",
    )(c, mod_w, mod_b.reshape(depth, 1, n))


FFN_CHUNK = 256


def _ffn_kernel(x_ref, g_ref, sh_ref, sc_ref, gt_ref, wg_ref, wu_ref, wd_ref, o_ref):
    x = x_ref[0]
    h = _norm_mod(x, g_ref[...], sh_ref[0], sc_ref[0]).astype(BF16)
    acc = jnp.zeros_like(x)
    for c in range(wg_ref.shape[1] // FFN_CHUNK):
        cs = slice(c * FFN_CHUNK, (c + 1) * FFN_CHUNK)
        gg = _dot(h, wg_ref[:, cs])
        uu = _dot(h, wu_ref[:, cs])
        a = (gg * _sigmoid(gg) * uu).astype(BF16)
        acc = acc + _dot(a, wd_ref[cs, :])
    o_ref[0] = x + (MACARON_WEIGHT * gt_ref[0]) * acc


def _ffn(x, g, shift, scale, gate, wg, wu, wd, layer, which, tm=512):
    b, s, d = x.shape
    f = wg.shape[-1]
    row = pl.BlockSpec((1, 1, d), lambda i, j: (i, 0, 0))
    return pl.pallas_call(
        _ffn_kernel,
        grid=(b, s // tm),
        in_specs=[
            pl.BlockSpec((1, tm, d), lambda i, j: (i, j, 0)),
            pl.BlockSpec((1, d), lambda i, j: (0, 0)),
            row, row, row,
            _resident((None, None, d, f), lambda i, j: (layer, which, 0, 0)),
            _resident((None, None, d, f), lambda i, j: (layer, which, 0, 0)),
            _resident((None, None, f, d), lambda i, j: (layer, which, 0, 0)),
        ],
        out_specs=pl.BlockSpec((1, tm, d), lambda i, j: (i, j, 0)),
        out_shape=jax.ShapeDtypeStruct(x.shape, x.dtype),
        compiler_params=_params(("arbitrary", "arbitrary")),
        name="ffn",
    )(x, g, shift, scale, gate, wg, wu, wd)


def _cumsum_rows(x):
    n = x.shape[0]
    row = lax.broadcasted_iota(jnp.int32, x.shape, 0)
    shift = 1
    while shift < n:
        x = x + jnp.where(row >= shift, pltpu.roll(x, shift, 0), 0.0)
        shift *= 2
    return x


N_SPLIT = 3
LOG2E = 1.4426950408889634


def _fox_in_kernel(x_ref, g_ref, sh_ref, sc_ref, wqk_ref, wvt_ref, wf_ref, bf_ref, qg_ref,
                   kg_ref, sel_ref, q_ref, ka_ref, vt_ref, carry_ref, *, n_heads):
    d = x_ref.shape[2]
    tm = x_ref.shape[1]

    @pl.when(pl.program_id(1) == 0)
    def _():
        carry_ref[...] = jnp.zeros_like(carry_ref)

    h = _norm_mod(x_ref[0], g_ref[...], sh_ref[0], sc_ref[0]).astype(BF16)
    q = _head_rms_norm(_dot(h, wqk_ref[:, 0:d]), qg_ref[...])
    q_ref[0] = (q * (LOG2E * HEAD_DIM ** -0.5)).astype(BF16)
    k = _head_rms_norm(_dot(h, wqk_ref[:, d:2 * d]), kg_ref[...]).astype(BF16)
    vt_ref[0] = _dot_nt(wvt_ref[...], h).astype(BF16)

    z = _dot(h, wf_ref[...]) + bf_ref[...]
    logf = jnp.minimum(z, 0.0) - jnp.log1p(jnp.exp(-jnp.abs(z)))
    lane = lax.broadcasted_iota(jnp.int32, logf.shape, 1)
    logf = jnp.where(lane < n_heads, logf, 0.0)
    cum = _cumsum_rows(logf) + carry_ref[...]
    carry_ref[...] = cum[tm - 1:tm, :]

    terms = []
    rest = cum * LOG2E
    for _ in range(N_SPLIT):
        t = rest.astype(BF16)
        terms.append(t)
        rest = rest - t.astype(F32)
    kx = _dot(jnp.concatenate(terms, axis=1), sel_ref[...]).astype(BF16)
    pieces = []
    for p in range(d // LANES):
        pieces += [k[:, p * LANES:(p + 1) * LANES], kx[:, p * LANES:(p + 1) * LANES]]
    ka_ref[0] = jnp.concatenate(pieces, axis=1)


def _fox_in(x, g, shift, scale, w_in, b_f, q_g, k_g, tm=512):
    b, s, d = x.shape
    n_heads = d // HEAD_DIM
    wqk = w_in[:, :2 * d].astype(BF16)
    wvt = w_in[:, 2 * d:3 * d].T.astype(BF16)
    wf = jnp.pad(w_in[:, 3 * d:], ((0, 0), (0, LANES - n_heads))).astype(BF16)
    bf = jnp.pad(b_f, (0, LANES - n_heads)).reshape(1, LANES)
    qg = jnp.tile(q_g, n_heads).reshape(1, d)
    kg = jnp.tile(k_g, n_heads).reshape(1, d)
    heads = jnp.arange(n_heads)
    sel = jnp.zeros((N_SPLIT * LANES, d), BF16)
    for t in range(N_SPLIT):
        sel = sel.at[t * LANES + heads, (heads // 2) * LANES + (heads % 2) * N_SPLIT + t].set(1.0)
    row = pl.BlockSpec((1, 1, d), lambda i, j: (i, 0, 0))
    full = lambda shp: pl.BlockSpec(shp, lambda i, j: (0, 0))
    res = lambda a: _resident(a.shape, lambda i, j: (0, 0))
    act = pl.BlockSpec((1, tm, d), lambda i, j: (i, j, 0))
    return pl.pallas_call(
        functools.partial(_fox_in_kernel, n_heads=n_heads),
        grid=(b, s // tm),
        in_specs=[act, full((1, d)), row, row, res(wqk), res(wvt), res(wf),
                  full((1, LANES)), full((1, d)), full((1, d)), res(sel)],
        out_specs=[act, pl.BlockSpec((1, tm, 2 * d), lambda i, j: (i, j, 0)),
                   pl.BlockSpec((1, d, tm), lambda i, j: (i, 0, j))],
        out_shape=[jax.ShapeDtypeStruct((b, s, d), BF16),
                   jax.ShapeDtypeStruct((b, s, 2 * d), BF16),
                   jax.ShapeDtypeStruct((b, d, s), BF16)],
        scratch_shapes=[pltpu.VMEM((1, LANES), F32)],
        compiler_params=_params(("arbitrary", "arbitrary")),
        name="fox_in",
    )(x, g, shift, scale, wqk, wvt, wf, bf, qg, kg, sel)


ONES_ROWS = 16


def _fox_attn_kernel(q_ref, ka_ref, vt_ref, o_ref, qa_ref, m_ref, acc_ref, s_ref, *, tq, strip):
    tk = tq
    n_tiles = q_ref.shape[1] // tq
    lane = lax.broadcasted_iota(jnp.int32, (1, LANES), 1)
    q2 = q_ref[0]
    zero = jnp.zeros_like(q2)
    for hh in range(2):
        head = (lane < HEAD_DIM) if hh == 0 else (lane >= HEAD_DIM)
        pick = jnp.logical_and(lane >= hh * N_SPLIT, lane < (hh + 1) * N_SPLIT)
        minus_one = jnp.broadcast_to(jnp.where(pick, -1.0, 0.0), q2.shape).astype(BF16)
        qa_ref[hh] = jnp.concatenate([jnp.where(head, q2, zero), minus_one], axis=1)

    ones_rows = jnp.ones((ONES_ROWS, tk), BF16)
    chains = [(hh, slice(st * strip, (st + 1) * strip), st)
              for hh in range(2) for st in range(tq // strip)]
    n = len(chains)

    def logits(c, i, j, diagonal):
        hh, _, st = chains[c]
        keys = pl.ds(pl.multiple_of(j * tk, tk), tk)
        qrys = pl.ds(pl.multiple_of(i * tq + st * strip, strip), strip)
        s = _dot_nt(ka_ref[0, keys, :], qa_ref[hh, qrys, :])
        if diagonal:
            key = lax.broadcasted_iota(jnp.int32, s.shape, 0)
            qry = lax.broadcasted_iota(jnp.int32, s.shape, 1) + st * strip
            s = jnp.where(key <= qry, s, -jnp.inf)
        return s

    def softmax_pv(c, j):
        hh, cols, _ = chains[c]
        s = s_ref[c]
        m_prev = m_ref[hh, :, cols]
        m_new = jnp.maximum(m_prev, jnp.max(s, axis=0, keepdims=True))
        p = jnp.exp2(s - m_new)
        alpha = jnp.exp2(m_prev - m_new)
        m_ref[hh, :, cols] = m_new
        keys = pl.ds(pl.multiple_of(j * tk, tk), tk)
        vt_ones = jnp.concatenate([vt_ref[0, :, keys], ones_rows], axis=0)
        acc_ref[hh, :, cols] = alpha * acc_ref[hh, :, cols] + _dot(vt_ones, p.astype(BF16))

    def step(j, i_next, j_next, diagonal_next):
        ahead = 2
        s_next = [logits(c, i_next, j_next, diagonal_next) for c in range(ahead)]
        for c in range(n):
            if c + ahead < n:
                s_next.append(logits(c + ahead, i_next, j_next, diagonal_next))
            softmax_pv(c, j)
            s_ref[c] = s_next[c]

    for c in range(n):
        s_ref[c] = logits(c, 0, 0, True)

    def tile(i, carry):
        m_ref[...] = jnp.full_like(m_ref, -jnp.inf)
        acc_ref[...] = jnp.zeros_like(acc_ref)

        def body(j, carry):
            step(j, i, j + 1, False)
            return carry

        lax.fori_loop(0, i - 1, body, 0)

        @pl.when(i > 0)
        def _():
            step(i - 1, i, i, True)

        step(i, jnp.minimum(i + 1, n_tiles - 1), 0, False)

        dim = lax.broadcasted_iota(jnp.int32, (LANES, 1), 0)
        o_heads = [acc_ref[hh, :LANES, :] * (1.0 / acc_ref[hh, LANES:LANES + 1, :])
                   for hh in range(2)]
        rows = pl.ds(pl.multiple_of(i * tq, tq), tq)
        o_ref[0, rows, :] = jnp.where(dim < HEAD_DIM, o_heads[0], o_heads[1]).T.astype(o_ref.dtype)
        return carry

    lax.fori_loop(0, n_tiles, tile, 0)


def _fox_attn(q, ka, vt, tq=512):
    b, s, d = q.shape
    qo = pl.BlockSpec((1, s, LANES), lambda bi, p: (bi, 0, p))
    return pl.pallas_call(
        functools.partial(_fox_attn_kernel, tq=tq, strip=MXU_DIM),
        grid=(b, d // LANES),
        in_specs=[qo, pl.BlockSpec((1, s, 2 * LANES), lambda bi, p: (bi, 0, p)),
                  pl.BlockSpec((1, LANES, s), lambda bi, p: (bi, p, 0))],
        out_specs=qo,
        out_shape=jax.ShapeDtypeStruct((b, s, d), BF16),
        scratch_shapes=[pltpu.VMEM((2, s, 2 * LANES), BF16),
                        pltpu.VMEM((2, 1, tq), F32),
                        pltpu.VMEM((2, LANES + ONES_ROWS, tq), F32),
                        pltpu.VMEM((2 * tq // MXU_DIM, tq, MXU_DIM), F32)],
        compiler_params=_params(("arbitrary", "arbitrary")),
        name="fox_attn",
    )(q, ka, vt)


def _out_proj_kernel(x_ref, o_ref, w_ref, gt_ref, out_ref):
    out_ref[0] = x_ref[0] + gt_ref[0] * _dot(o_ref[0], w_ref[...])


def _out_proj(x, o, w_out, gate, tm=512):
    b, s, d = x.shape
    act = pl.BlockSpec((1, tm, d), lambda i, j: (i, j, 0))
    return pl.pallas_call(
        _out_proj_kernel,
        grid=(b, s // tm),
        in_specs=[act, act, _resident(w_out.shape, lambda i, j: (0, 0)),
                  pl.BlockSpec((1, 1, d), lambda i, j: (i, 0, 0))],
        out_specs=act,
        out_shape=jax.ShapeDtypeStruct(x.shape, x.dtype),
        compiler_params=_params(("arbitrary", "arbitrary")),
        name="out_proj",
    )(x, o, w_out.astype(BF16), gate)


def _rope_kernel(pos_ref, invf_ref, cos_ref, sin_ref):
    ang = pos_ref[0].astype(F32) * invf_ref[...]
    d = lax.broadcasted_iota(jnp.int32, ang.shape, 1) % HEAD_DIM
    half = ROPE_DIM // 2
    cos_ref[0] = jnp.where(d < ROPE_DIM, jnp.cos(ang), 1.0)
    sn = jnp.sin(ang)
    sin_ref[0] = jnp.where(d < half, -sn, jnp.where(d < ROPE_DIM, sn, 0.0))


def _rope_tables(positions, tm=512):
    b, s = positions.shape
    half = ROPE_DIM // 2
    inv_freq = ROPE_THETA ** (-(jnp.arange(half, dtype=F32) * 2.0 / ROPE_DIM))
    d = jnp.arange(LANES) % HEAD_DIM
    invf = jnp.where(d < ROPE_DIM, inv_freq[d % half], 0.0).reshape(1, LANES)
    tab = pl.BlockSpec((1, tm, LANES), lambda i, j: (i, j, 0))
    return pl.pallas_call(
        _rope_kernel,
        grid=(b, s // tm),
        in_specs=[pl.BlockSpec((1, tm, 1), lambda i, j: (i, j, 0)),
                  pl.BlockSpec((1, LANES), lambda i, j: (0, 0))],
        out_specs=[tab, tab],
        out_shape=[jax.ShapeDtypeStruct((b, s, LANES), F32)] * 2,
        compiler_params=_params(("arbitrary", "arbitrary")),
        name="rope_tables",
    )(positions.reshape(b, s, 1), invf)


def _rope(x, cos, sin):
    half = ROPE_DIM // 2
    d = lax.broadcasted_iota(jnp.int32, (1, LANES), 1) % HEAD_DIM
    outs = []
    for c in range(x.shape[1] // LANES):
        xc = x[:, c * LANES:(c + 1) * LANES]
        partner = jnp.where(d < half, pltpu.roll(xc, LANES - half, 1), pltpu.roll(xc, half, 1))
        outs.append(xc * cos + partner * sin)
    return jnp.concatenate(outs, axis=1)


def _dil_in_kernel(x_ref, g_ref, sh_ref, sc_ref, w_ref, qg_ref, kg_ref, cos_ref, sin_ref,
                   *refs):
    out_refs = refs
    tm, d = x_ref.shape[1:]
    h = _norm_mod(x_ref[0], g_ref[...], sh_ref[0], sc_ref[0]).astype(BF16)
    new = lax.broadcasted_iota(jnp.int32, (tm, tm), 0)
    old = lax.broadcasted_iota(jnp.int32, (tm, tm), 1)
    for grp, (_, dilation) in enumerate(DIL_CONFIGS):
        n = tm // dilation
        if dilation == 1:
            h_g, cos, sin = h, cos_ref[0], sin_ref[0]
        else:
            perm = jnp.where(old == (new % n) * dilation + new // n, 1.0, 0.0).astype(BF16)
            h_g = _dot(perm, h).astype(BF16)
            strided = lambda ref: jnp.concatenate(
                [ref[0, pl.ds(r, n, stride=dilation), :] for r in range(dilation)], axis=0)
            cos, sin = strided(cos_ref), strided(sin_ref)
        base = grp * 3 * d
        q = _head_rms_norm(_dot(h_g, w_ref[:, base:base + d]), qg_ref[grp:grp + 1, :])
        k = _head_rms_norm(_dot(h_g, w_ref[:, base + d:base + 2 * d]), kg_ref[grp:grp + 1, :])
        vals = (_rope(q, cos, sin) * (LOG2E * HEAD_DIM ** -0.5), _rope(k, cos, sin),
                _dot(h_g, w_ref[:, base + 2 * d:base + 3 * d]))
        for out_ref, val in zip(out_refs[3 * grp:3 * grp + 3], vals):
            val = val.astype(BF16)
            for r in range(dilation):
                out_ref[0, r] = val[r * n:(r + 1) * n]


def _dil_in(x, g, shift, scale, w_in, q_g, k_g, cos, sin, tm=256):
    b, s, d = x.shape
    n_groups = len(DIL_CONFIGS)
    n_heads = d // HEAD_DIM
    qg = jnp.tile(q_g, (1, n_heads))
    kg = jnp.tile(k_g, (1, n_heads))
    row = pl.BlockSpec((1, 1, d), lambda i, j: (i, 0, 0))
    full = lambda shp: pl.BlockSpec(shp, lambda i, j: (0, 0))
    act = pl.BlockSpec((1, tm, d), lambda i, j: (i, j, 0))
    tab = pl.BlockSpec((1, tm, LANES), lambda i, j: (i, j, 0))
    w = w_in.astype(BF16)
    out_specs, out_shape = [], []
    for _, dil in DIL_CONFIGS:
        out_specs += [pl.BlockSpec((1, dil, tm // dil, d), lambda i, j: (i, 0, j, 0))] * 3
        out_shape += [jax.ShapeDtypeStruct((b, dil, s // dil, d), BF16)] * 3
    return pl.pallas_call(
        _dil_in_kernel,
        grid=(b, s // tm),
        in_specs=[act, full((1, d)), row, row, _resident(w.shape, lambda i, j: (0, 0)),
                  full((n_groups, d)), full((n_groups, d)), tab, tab],
        out_specs=out_specs,
        out_shape=out_shape,
        compiler_params=_params(("arbitrary", "arbitrary")),
        name="dil_in",
    )(x, g, shift, scale, w, qg, kg, cos, sin)


def _dil_attn_kernel(q_ref, kc_ref, kp_ref, vc_ref, vp_ref, o_ref, lse_ref, lse_t_ref):
    w = DIL_WINDOW_KEYS
    tq, d = q_ref.shape[2:]
    n_heads = d // HEAD_DIM
    i = pl.program_id(2)
    lane = lax.broadcasted_iota(jnp.int32, (1, LANES), 1)
    key = lax.broadcasted_iota(jnp.int32, (2 * w, 2 * w), 0)
    qry = lax.broadcasted_iota(jnp.int32, (2 * w, 2 * w), 1) % w
    band = jnp.logical_and(key >= qry, key <= qry + w)
    bias_band = jnp.where(band, 0.0, -jnp.inf)
    in_seq = jnp.logical_or(key >= w, i > 0)
    bias_first = jnp.where(jnp.logical_and(band, in_seq), 0.0, -jnp.inf)
    dim = lax.broadcasted_iota(jnp.int32, (LANES, 1), 0)

    chains = [(p, sub) for p in range(d // LANES) for sub in range(tq // w)]
    zero = jnp.zeros((w, LANES), q_ref.dtype)

    def window(cur_ref, prev_ref, p, sub):
        cs = slice(p * LANES, (p + 1) * LANES)
        if sub == 0:
            return jnp.concatenate([prev_ref[0, 0, :, cs], cur_ref[0, 0, :w, cs]], axis=0)
        return cur_ref[0, 0, (sub - 1) * w:(sub + 1) * w, cs]

    def logits(c):
        p, sub = chains[c]
        qs = q_ref[0, 0, sub * w:(sub + 1) * w, p * LANES:(p + 1) * LANES]
        q_both = jnp.concatenate([jnp.where(lane < HEAD_DIM, qs, zero),
                                  jnp.where(lane >= HEAD_DIM, qs, zero)], axis=0)
        s = _dot_nt(window(kc_ref, kp_ref, p, sub), q_both)
        return s + (bias_first if sub == 0 else bias_band)

    def finish(c, s):
        p, sub = chains[c]
        rows = slice(sub * w, (sub + 1) * w)
        m = jnp.max(s, axis=0, keepdims=True)
        e = jnp.exp2(s - m)
        l = jnp.sum(e, axis=0, keepdims=True)
        o_t = lax.dot_general(window(vc_ref, vp_ref, p, sub), e.astype(BF16),
                              (((0,), (0,)), ((), ())),
                              preferred_element_type=F32) * (1.0 / l)
        o_ref[0, 0, rows, p * LANES:(p + 1) * LANES] = (
            jnp.where(dim < HEAD_DIM, o_t[:, :w], o_t[:, w:]).T)
        lse = m + jnp.log(l) * LOG2E
        lse_t_ref[2 * p:2 * p + 1, rows] = lse[:, :w]
        lse_t_ref[2 * p + 1:2 * p + 2, rows] = lse[:, w:]

    lse_t_ref[...] = jnp.zeros_like(lse_t_ref)
    ahead = 2
    pending = [logits(c) for c in range(ahead)]
    for c in range(len(chains)):
        if c + ahead < len(chains):
            pending.append(logits(c + ahead))
        finish(c, pending[c])
    lse_ref[0, 0] = lse_t_ref[...].T


def _dil_attn(q, k, v, tq=256):
    b, dilation, sub_len, d = q.shape
    w = DIL_WINDOW_KEYS
    tq = min(tq, sub_len)
    cur = pl.BlockSpec((1, 1, tq, d), lambda bi, r, i: (bi, r, i, 0))
    prev = pl.BlockSpec((1, 1, w, d),
                        lambda bi, r, i: (bi, r, jnp.maximum(i * (tq // w) - 1, 0), 0))
    return pl.pallas_call(
        _dil_attn_kernel,
        grid=(b, dilation, sub_len // tq),
        in_specs=[cur, cur, prev, cur, prev],
        out_specs=[cur, pl.BlockSpec((1, 1, tq, LANES), lambda bi, r, i: (bi, r, i, 0))],
        out_shape=[jax.ShapeDtypeStruct(q.shape, F32),
                   jax.ShapeDtypeStruct((b, dilation, sub_len, LANES), F32)],
        scratch_shapes=[pltpu.VMEM((LANES, tq), F32)],
        compiler_params=_params(("arbitrary", "arbitrary", "arbitrary")),
        name="dil_attn",
    )(q, k, k, v, v)


def _dil_out_kernel(x_ref, o0_ref, o1_ref, o2_ref, l0_ref, l1_ref, l2_ref, w_ref, gt_ref,
                    out_ref, o_stage_ref, l_stage_ref):
    d = x_ref.shape[2]

    def token_order(ref, stage_ref, slot):
        dilation, n, width = ref.shape[1:]
        if dilation == 1:
            return ref[0, 0]
        n_tiles = width // LANES
        for r in range(dilation):
            for t in range(n_tiles):
                stage_ref[slot * n_tiles + t, pl.ds(r, n, stride=dilation), :] = (
                    ref[0, r, :, t * LANES:(t + 1) * LANES])
        return jnp.concatenate([stage_ref[slot * n_tiles + t] for t in range(n_tiles)], axis=1)

    lses = [token_order(ref, l_stage_ref, g) for g, ref in enumerate((l0_ref, l1_ref, l2_ref))]
    outs = [token_order(ref, o_stage_ref, g) for g, ref in enumerate((o0_ref, o1_ref, o2_ref))]
    m = jnp.maximum(jnp.maximum(lses[0], lses[1]), lses[2])
    es = [jnp.exp2(l - m) for l in lses]
    inv = 1.0 / (es[0] + es[1] + es[2])
    r = lax.broadcasted_iota(jnp.int32, (LANES, d), 0)
    c = lax.broadcasted_iota(jnp.int32, (LANES, d), 1) // HEAD_DIM
    expand = jnp.where(r == c, 1.0, 0.0).astype(BF16)
    merged = jnp.zeros(x_ref.shape[1:], F32)
    for e, o in zip(es, outs):
        alpha = e * inv
        hi = alpha.astype(BF16)
        lo = (alpha - hi.astype(F32)).astype(BF16)
        merged = merged + (_dot(hi, expand) + _dot(lo, expand)) * o
    out_ref[0] = x_ref[0] + gt_ref[0] * _dot(merged.astype(BF16), w_ref[...])


def _dil_out(x, outs, lses, w_out, gate, tm=512):
    b, s, d = x.shape
    act = pl.BlockSpec((1, tm, d), lambda i, j: (i, j, 0))
    grouped = lambda a: pl.BlockSpec((1, a.shape[1], tm // a.shape[1], a.shape[3]),
                                     lambda i, j: (i, 0, j, 0))
    return pl.pallas_call(
        _dil_out_kernel,
        grid=(b, s // tm),
        in_specs=[act] + [grouped(a) for a in outs] + [grouped(a) for a in lses]
        + [_resident(w_out.shape, lambda i, j: (0, 0)),
           pl.BlockSpec((1, 1, d), lambda i, j: (i, 0, 0))],
        out_specs=act,
        out_shape=jax.ShapeDtypeStruct(x.shape, x.dtype),
        scratch_shapes=[pltpu.VMEM((len(outs) * d // LANES, tm, LANES), F32),
                        pltpu.VMEM((len(lses), tm, LANES), F32)],
        compiler_params=_params(("arbitrary", "arbitrary")),
        name="dil_out",
    )(x, *outs, *lses, w_out.astype(BF16), gate)


def kernel(x, c, positions, mod_w, mod_b, norm_g, ffn_w_gate, ffn_w_up, ffn_w_down,
           fox_w_in, fox_b_f, fox_q_g, fox_k_g, fox_w_out,
           dil_w_in, dil_q_g, dil_k_g, dil_w_out):
    b, s, d = x.shape
    depth = mod_w.shape[0]
    mod = _modulation(c, mod_w, mod_b).reshape(depth, b, N_SUBLAYERS, 3, 1, d)
    wg = ffn_w_gate.astype(BF16)
    wu = ffn_w_up.astype(BF16)
    wd = ffn_w_down.astype(BF16)
    cos, sin = _rope_tables(positions)

    for i in range(depth):
        shift = lambda sub: mod[i, :, sub, 0]
        scale = lambda sub: mod[i, :, sub, 1]
        gate = lambda sub: mod[i, :, sub, 2]
        g = lambda sub: norm_g[i, sub].reshape(1, d)

        x = _ffn(x, g(0), shift(0), scale(0), gate(0), wg, wu, wd, i, 0)
        j = i // 2
        if i % 2 == 0:
            q, ka, vt = _fox_in(x, g(1), shift(1), scale(1), fox_w_in[j], fox_b_f[j],
                                fox_q_g[j], fox_k_g[j])
            o = _fox_attn(q, ka, vt)
            x = _out_proj(x, o, fox_w_out[j], gate(1))
        else:
            qkv = _dil_in(x, g(1), shift(1), scale(1), dil_w_in[j], dil_q_g[j], dil_k_g[j],
                          cos, sin)
            outs, lses = [], []
            for grp in range(len(DIL_CONFIGS)):
                o, lse = _dil_attn(qkv[3 * grp], qkv[3 * grp + 1], qkv[3 * grp + 2])
                outs.append(o)
                lses.append(lse)
            x = _dil_out(x, outs, lses, dil_w_out[j], gate(1))
        x = _ffn(x, g(2), shift(2), scale(2), gate(2), wg, wu, wd, i, 1)
    return x
```

```python
import functools

import jax
import jax.numpy as jnp
from jax import lax
from jax.experimental import pallas as pl
from jax.experimental.pallas import tpu as pltpu

HEAD_DIM = 64
ROPE_DIM = 16
ROPE_THETA = 500000.0
EPS = 1e-6
MACARON_WEIGHT = 0.5
N_SUBLAYERS = 3
DIL_CONFIGS = ((128, 1), (512, 4), (2048, 16))
DIL_WINDOW_KEYS = 128

LANES = 128
MXU_DIM = 256
VMEM_LIMIT = 56 * 1024 * 1024

F32 = jnp.float32
BF16 = jnp.bfloat16


def _params(semantics, vmem=VMEM_LIMIT):
    return pltpu.CompilerParams(dimension_semantics=semantics, vmem_limit_bytes=vmem)


def _resident(block_shape, index_map):
    return pl.BlockSpec(block_shape, index_map, pipeline_mode=pl.Buffered(1))


def _dot(a, b):
    return jnp.dot(a, b, preferred_element_type=F32)


def _dot_nt(a, b):
    return lax.dot_general(a, b, (((1,), (1,)), ((), ())), preferred_element_type=F32)


def _sigmoid(x):
    return 1.0 / (1.0 + jnp.exp(-x))


def _norm_mod(x, g, shift, scale):
    ms = jnp.mean(x * x, axis=-1, keepdims=True)
    y = x * lax.rsqrt(ms + EPS) * g
    return y * (1.0 + scale) + shift


def _head_block_ones(n):
    r = lax.broadcasted_iota(jnp.int32, (n, n), 0) // HEAD_DIM
    c = lax.broadcasted_iota(jnp.int32, (n, n), 1) // HEAD_DIM
    return jnp.where(r == c, 1.0, 0.0).astype(BF16)


def _head_rms_norm(x, gain):
    ones = _head_block_ones(MXU_DIM)
    outs = []
    for c in range(x.shape[1] // MXU_DIM):
        xc = x[:, c * MXU_DIM:(c + 1) * MXU_DIM]
        ssq = _dot((xc * xc).astype(BF16), ones)
        r = lax.rsqrt(ssq * (1.0 / HEAD_DIM) + EPS)
        outs.append(xc * r * gain[:, c * MXU_DIM:(c + 1) * MXU_DIM])
    return jnp.concatenate(outs, axis=1)


def _mod_kernel(c_ref, w_ref, b_ref, o_ref):
    c = c_ref[...]
    c_act = c * _sigmoid(c)
    o_ref[0] = jnp.dot(c_act, w_ref[0], preferred_element_type=F32,
                       precision=lax.Precision.HIGHEST) + b_ref[0]


def _modulation(c, mod_w, mod_b):
    depth, d, n = mod_w.shape
    b = c.shape[0]
    tn = 1024
    return pl.pallas_call(
        _mod_kernel,
        grid=(depth, n // tn),
        in_specs=[
            pl.BlockSpec((b, d), lambda i, j: (0, 0)),
            pl.BlockSpec((1, d, tn), lambda i, j: (i, 0, j)),
            pl.BlockSpec((1, 1, tn), lambda i, j: (i, 0, j)),
        ],
        out_specs=pl.BlockSpec((1, b, tn), lambda i, j: (i, 0, j)),
        out_shape=jax.ShapeDtypeStruct((depth, b, n), F32),
        compiler_params=_params(("arbitrary", "arbitrary")),
        name="modulation",
    )(c, mod_w, mod_b.reshape(depth, 1, n))


FFN_CHUNK = 256


def _ffn_kernel(x_ref, g_ref, sh_ref, sc_ref, gt_ref, wg_ref, wu_ref, wd_ref, o_ref):
    x = x_ref[0]
    h = _norm_mod(x, g_ref[...], sh_ref[0], sc_ref[0]).astype(BF16)
    acc = jnp.zeros_like(x)
    for c in range(wg_ref.shape[1] // FFN_CHUNK):
        cs = slice(c * FFN_CHUNK, (c + 1) * FFN_CHUNK)
        gg = _dot(h, wg_ref[:, cs])
        uu = _dot(h, wu_ref[:, cs])
        a = (gg * _sigmoid(gg) * uu).astype(BF16)
        acc = acc + _dot(a, wd_ref[cs, :])
    o_ref[0] = x + (MACARON_WEIGHT * gt_ref[0]) * acc


def _ffn(x, g, shift, scale, gate, wg, wu, wd, layer, which, tm=512):
    b, s, d = x.shape
    f = wg.shape[-1]
    row = pl.BlockSpec((1, 1, d), lambda i, j: (i, 0, 0))
    return pl.pallas_call(
        _ffn_kernel,
        grid=(b, s // tm),
        in_specs=[
            pl.BlockSpec((1, tm, d), lambda i, j: (i, j, 0)),
            pl.BlockSpec((1, d), lambda i, j: (0, 0)),
            row, row, row,
            _resident((None, None, d, f), lambda i, j: (layer, which, 0, 0)),
            _resident((None, None, d, f), lambda i, j: (layer, which, 0, 0)),
            _resident((None, None, f, d), lambda i, j: (layer, which, 0, 0)),
        ],
        out_specs=pl.BlockSpec((1, tm, d), lambda i, j: (i, j, 0)),
        out_shape=jax.ShapeDtypeStruct(x.shape, x.dtype),
        compiler_params=_params(("arbitrary", "arbitrary")),
        name="ffn",
    )(x, g, shift, scale, gate, wg, wu, wd)


def _cumsum_rows(x):
    n = x.shape[0]
    row = lax.broadcasted_iota(jnp.int32, x.shape, 0)
    shift = 1
    while shift < n:
        x = x + jnp.where(row >= shift, pltpu.roll(x, shift, 0), 0.0)
        shift *= 2
    return x


N_SPLIT = 3
LOG2E = 1.4426950408889634


def _fox_in_kernel(x_ref, g_ref, sh_ref, sc_ref, wqk_ref, wvt_ref, wf_ref, bf_ref, qg_ref,
                   kg_ref, sel_ref, q_ref, ka_ref, vt_ref, carry_ref, *, n_heads):
    d = x_ref.shape[2]
    tm = x_ref.shape[1]

    @pl.when(pl.program_id(1) == 0)
    def _():
        carry_ref[...] = jnp.zeros_like(carry_ref)

    h = _norm_mod(x_ref[0], g_ref[...], sh_ref[0], sc_ref[0]).astype(BF16)
    q = _head_rms_norm(_dot(h, wqk_ref[:, 0:d]), qg_ref[...])
    q_ref[0] = (q * (LOG2E * HEAD_DIM ** -0.5)).astype(BF16)
    k = _head_rms_norm(_dot(h, wqk_ref[:, d:2 * d]), kg_ref[...]).astype(BF16)
    vt_ref[0] = _dot_nt(wvt_ref[...], h).astype(BF16)

    z = _dot(h, wf_ref[...]) + bf_ref[...]
    logf = jnp.minimum(z, 0.0) - jnp.log1p(jnp.exp(-jnp.abs(z)))
    lane = lax.broadcasted_iota(jnp.int32, logf.shape, 1)
    logf = jnp.where(lane < n_heads, logf, 0.0)
    cum = _cumsum_rows(logf) + carry_ref[...]
    carry_ref[...] = cum[tm - 1:tm, :]

    terms = []
    rest = cum * LOG2E
    for _ in range(N_SPLIT):
        t = rest.astype(BF16)
        terms.append(t)
        rest = rest - t.astype(F32)
    kx = _dot(jnp.concatenate(terms, axis=1), sel_ref[...]).astype(BF16)
    pieces = []
    for p in range(d // LANES):
        pieces += [k[:, p * LANES:(p + 1) * LANES], kx[:, p * LANES:(p + 1) * LANES]]
    ka_ref[0] = jnp.concatenate(pieces, axis=1)


def _fox_in(x, g, shift, scale, w_in, b_f, q_g, k_g, tm=512):
    b, s, d = x.shape
    n_heads = d // HEAD_DIM
    wqk = w_in[:, :2 * d].astype(BF16)
    wvt = w_in[:, 2 * d:3 * d].T.astype(BF16)
    wf = jnp.pad(w_in[:, 3 * d:], ((0, 0), (0, LANES - n_heads))).astype(BF16)
    bf = jnp.pad(b_f, (0, LANES - n_heads)).reshape(1, LANES)
    qg = jnp.tile(q_g, n_heads).reshape(1, d)
    kg = jnp.tile(k_g, n_heads).reshape(1, d)
    heads = jnp.arange(n_heads)
    sel = jnp.zeros((N_SPLIT * LANES, d), BF16)
    for t in range(N_SPLIT):
        sel = sel.at[t * LANES + heads, (heads // 2) * LANES + (heads % 2) * N_SPLIT + t].set(1.0)
    row = pl.BlockSpec((1, 1, d), lambda i, j: (i, 0, 0))
    full = lambda shp: pl.BlockSpec(shp, lambda i, j: (0, 0))
    res = lambda a: _resident(a.shape, lambda i, j: (0, 0))
    act = pl.BlockSpec((1, tm, d), lambda i, j: (i, j, 0))
    return pl.pallas_call(
        functools.partial(_fox_in_kernel, n_heads=n_heads),
        grid=(b, s // tm),
        in_specs=[act, full((1, d)), row, row, res(wqk), res(wvt), res(wf),
                  full((1, LANES)), full((1, d)), full((1, d)), res(sel)],
        out_specs=[act, pl.BlockSpec((1, tm, 2 * d), lambda i, j: (i, j, 0)),
                   pl.BlockSpec((1, d, tm), lambda i, j: (i, 0, j))],
        out_shape=[jax.ShapeDtypeStruct((b, s, d), BF16),
                   jax.ShapeDtypeStruct((b, s, 2 * d), BF16),
                   jax.ShapeDtypeStruct((b, d, s), BF16)],
        scratch_shapes=[pltpu.VMEM((1, LANES), F32)],
        compiler_params=_params(("arbitrary", "arbitrary")),
        name="fox_in",
    )(x, g, shift, scale, wqk, wvt, wf, bf, qg, kg, sel)


ONES_ROWS = 16


def _fox_attn_kernel(q_ref, ka_ref, vt_ref, o_ref, qa_ref, m_ref, acc_ref, s_ref, *, tq, strip):
    tk = tq
    n_tiles = q_ref.shape[1] // tq
    lane = lax.broadcasted_iota(jnp.int32, (1, LANES), 1)
    q2 = q_ref[0]
    zero = jnp.zeros_like(q2)
    for hh in range(2):
        head = (lane < HEAD_DIM) if hh == 0 else (lane >= HEAD_DIM)
        pick = jnp.logical_and(lane >= hh * N_SPLIT, lane < (hh + 1) * N_SPLIT)
        minus_one = jnp.broadcast_to(jnp.where(pick, -1.0, 0.0), q2.shape).astype(BF16)
        qa_ref[hh] = jnp.concatenate([jnp.where(head, q2, zero), minus_one], axis=1)

    ones_rows = jnp.ones((ONES_ROWS, tk), BF16)
    chains = [(hh, slice(st * strip, (st + 1) * strip), st)
              for hh in range(2) for st in range(tq // strip)]
    n = len(chains)

    def logits(c, i, j, diagonal):
        hh, _, st = chains[c]
        keys = pl.ds(pl.multiple_of(j * tk, tk), tk)
        qrys = pl.ds(pl.multiple_of(i * tq + st * strip, strip), strip)
        s = _dot_nt(ka_ref[0, keys, :], qa_ref[hh, qrys, :])
        if diagonal:
            key = lax.broadcasted_iota(jnp.int32, s.shape, 0)
            qry = lax.broadcasted_iota(jnp.int32, s.shape, 1) + st * strip
            s = jnp.where(key <= qry, s, -jnp.inf)
        return s

    def softmax_pv(c, j):
        hh, cols, _ = chains[c]
        s = s_ref[c]
        m_prev = m_ref[hh, :, cols]
        m_new = jnp.maximum(m_prev, jnp.max(s, axis=0, keepdims=True))
        p = jnp.exp2(s - m_new)
        alpha = jnp.exp2(m_prev - m_new)
        m_ref[hh, :, cols] = m_new
        keys = pl.ds(pl.multiple_of(j * tk, tk), tk)
        vt_ones = jnp.concatenate([vt_ref[0, :, keys], ones_rows], axis=0)
        acc_ref[hh, :, cols] = alpha * acc_ref[hh, :, cols] + _dot(vt_ones, p.astype(BF16))

    def step(j, i_next, j_next, diagonal_next):
        ahead = 2
        s_next = [logits(c, i_next, j_next, diagonal_next) for c in range(ahead)]
        for c in range(n):
            if c + ahead < n:
                s_next.append(logits(c + ahead, i_next, j_next, diagonal_next))
            softmax_pv(c, j)
            s_ref[c] = s_next[c]

    for c in range(n):
        s_ref[c] = logits(c, 0, 0, True)

    def tile(i, carry):
        m_ref[...] = jnp.full_like(m_ref, -jnp.inf)
        acc_ref[...] = jnp.zeros_like(acc_ref)

        n_plain = i - 1

        def body(t, carry):
            step(2 * t, i, 2 * t + 1, False)
            step(2 * t + 1, i, 2 * t + 2, False)
            return carry

        lax.fori_loop(0, n_plain // 2, body, 0)

        @pl.when(jnp.logical_and(n_plain > 0, n_plain % 2 == 1))
        def _():
            step(i - 2, i, i - 1, False)

        i_next = jnp.minimum(i + 1, n_tiles - 1)

        @pl.when(i > 0)
        def _():
            step(i - 1, i, i, True)
            step(i, i_next, 0, False)

        @pl.when(i == 0)
        def _():
            step(i, i_next, 0, False)

        dim = lax.broadcasted_iota(jnp.int32, (LANES, 1), 0)
        o_heads = [acc_ref[hh, :LANES, :] * (1.0 / acc_ref[hh, LANES:LANES + 1, :])
                   for hh in range(2)]
        rows = pl.ds(pl.multiple_of(i * tq, tq), tq)
        o_ref[0, rows, :] = jnp.where(dim < HEAD_DIM, o_heads[0], o_heads[1]).T.astype(o_ref.dtype)
        return carry

    lax.fori_loop(0, n_tiles, tile, 0)


def _fox_attn(q, ka, vt, tq=512):
    b, s, d = q.shape
    qo = pl.BlockSpec((1, s, LANES), lambda bi, p: (bi, 0, p))
    return pl.pallas_call(
        functools.partial(_fox_attn_kernel, tq=tq, strip=MXU_DIM),
        grid=(b, d // LANES),
        in_specs=[qo, pl.BlockSpec((1, s, 2 * LANES), lambda bi, p: (bi, 0, p)),
                  pl.BlockSpec((1, LANES, s), lambda bi, p: (bi, p, 0))],
        out_specs=qo,
        out_shape=jax.ShapeDtypeStruct((b, s, d), BF16),
        scratch_shapes=[pltpu.VMEM((2, s, 2 * LANES), BF16),
                        pltpu.VMEM((2, 1, tq), F32),
                        pltpu.VMEM((2, LANES + ONES_ROWS, tq), F32),
                        pltpu.VMEM((2 * tq // MXU_DIM, tq, MXU_DIM), F32)],
        compiler_params=_params(("arbitrary", "arbitrary")),
        name="fox_attn",
    )(q, ka, vt)


def _out_proj_kernel(x_ref, o_ref, w_ref, gt_ref, out_ref):
    out_ref[0] = x_ref[0] + gt_ref[0] * _dot(o_ref[0], w_ref[...])


def _out_proj(x, o, w_out, gate, tm=512):
    b, s, d = x.shape
    act = pl.BlockSpec((1, tm, d), lambda i, j: (i, j, 0))
    return pl.pallas_call(
        _out_proj_kernel,
        grid=(b, s // tm),
        in_specs=[act, act, _resident(w_out.shape, lambda i, j: (0, 0)),
                  pl.BlockSpec((1, 1, d), lambda i, j: (i, 0, 0))],
        out_specs=act,
        out_shape=jax.ShapeDtypeStruct(x.shape, x.dtype),
        compiler_params=_params(("arbitrary", "arbitrary")),
        name="out_proj",
    )(x, o, w_out.astype(BF16), gate)


def _rope_kernel(pos_ref, invf_ref, cos_ref, sin_ref):
    ang = pos_ref[0].astype(F32) * invf_ref[...]
    d = lax.broadcasted_iota(jnp.int32, ang.shape, 1) % HEAD_DIM
    half = ROPE_DIM // 2
    cos_ref[0] = jnp.where(d < ROPE_DIM, jnp.cos(ang), 1.0)
    sn = jnp.sin(ang)
    sin_ref[0] = jnp.where(d < half, -sn, jnp.where(d < ROPE_DIM, sn, 0.0))


def _rope_tables(positions, tm=512):
    b, s = positions.shape
    half = ROPE_DIM // 2
    inv_freq = ROPE_THETA ** (-(jnp.arange(half, dtype=F32) * 2.0 / ROPE_DIM))
    d = jnp.arange(LANES) % HEAD_DIM
    invf = jnp.where(d < ROPE_DIM, inv_freq[d % half], 0.0).reshape(1, LANES)
    tab = pl.BlockSpec((1, tm, LANES), lambda i, j: (i, j, 0))
    return pl.pallas_call(
        _rope_kernel,
        grid=(b, s // tm),
        in_specs=[pl.BlockSpec((1, tm, 1), lambda i, j: (i, j, 0)),
                  pl.BlockSpec((1, LANES), lambda i, j: (0, 0))],
        out_specs=[tab, tab],
        out_shape=[jax.ShapeDtypeStruct((b, s, LANES), F32)] * 2,
        compiler_params=_params(("arbitrary", "arbitrary")),
        name="rope_tables",
    )(positions.reshape(b, s, 1), invf)


def _rope(x, cos, sin):
    half = ROPE_DIM // 2
    d = lax.broadcasted_iota(jnp.int32, (1, LANES), 1) % HEAD_DIM
    outs = []
    for c in range(x.shape[1] // LANES):
        xc = x[:, c * LANES:(c + 1) * LANES]
        partner = jnp.where(d < half, pltpu.roll(xc, LANES - half, 1), pltpu.roll(xc, half, 1))
        outs.append(xc * cos + partner * sin)
    return jnp.concatenate(outs, axis=1)


def _dil_in_kernel(x_ref, g_ref, sh_ref, sc_ref, w_ref, qg_ref, kg_ref, cos_ref, sin_ref,
                   *refs):
    out_refs = refs
    tm, d = x_ref.shape[1:]
    h = _norm_mod(x_ref[0], g_ref[...], sh_ref[0], sc_ref[0]).astype(BF16)
    new = lax.broadcasted_iota(jnp.int32, (tm, tm), 0)
    old = lax.broadcasted_iota(jnp.int32, (tm, tm), 1)
    for grp, (_, dilation) in enumerate(DIL_CONFIGS):
        n = tm // dilation
        if dilation == 1:
            h_g, cos, sin = h, cos_ref[0], sin_ref[0]
        else:
            perm = jnp.where(old == (new % n) * dilation + new // n, 1.0, 0.0).astype(BF16)
            h_g = _dot(perm, h).astype(BF16)
            strided = lambda ref: jnp.concatenate(
                [ref[0, pl.ds(r, n, stride=dilation), :] for r in range(dilation)], axis=0)
            cos, sin = strided(cos_ref), strided(sin_ref)
        base = grp * 3 * d
        q = _head_rms_norm(_dot(h_g, w_ref[:, base:base + d]), qg_ref[grp:grp + 1, :])
        k = _head_rms_norm(_dot(h_g, w_ref[:, base + d:base + 2 * d]), kg_ref[grp:grp + 1, :])
        vals = (_rope(q, cos, sin) * (LOG2E * HEAD_DIM ** -0.5), _rope(k, cos, sin),
                _dot(h_g, w_ref[:, base + 2 * d:base + 3 * d]))
        for out_ref, val in zip(out_refs[3 * grp:3 * grp + 3], vals):
            val = val.astype(BF16)
            for r in range(dilation):
                out_ref[0, r] = val[r * n:(r + 1) * n]


def _dil_in(x, g, shift, scale, w_in, q_g, k_g, cos, sin, tm=256):
    b, s, d = x.shape
    n_groups = len(DIL_CONFIGS)
    n_heads = d // HEAD_DIM
    qg = jnp.tile(q_g, (1, n_heads))
    kg = jnp.tile(k_g, (1, n_heads))
    row = pl.BlockSpec((1, 1, d), lambda i, j: (i, 0, 0))
    full = lambda shp: pl.BlockSpec(shp, lambda i, j: (0, 0))
    act = pl.BlockSpec((1, tm, d), lambda i, j: (i, j, 0))
    tab = pl.BlockSpec((1, tm, LANES), lambda i, j: (i, j, 0))
    w = w_in.astype(BF16)
    out_specs, out_shape = [], []
    for _, dil in DIL_CONFIGS:
        out_specs += [pl.BlockSpec((1, dil, tm // dil, d), lambda i, j: (i, 0, j, 0))] * 3
        out_shape += [jax.ShapeDtypeStruct((b, dil, s // dil, d), BF16)] * 3
    return pl.pallas_call(
        _dil_in_kernel,
        grid=(b, s // tm),
        in_specs=[act, full((1, d)), row, row, _resident(w.shape, lambda i, j: (0, 0)),
                  full((n_groups, d)), full((n_groups, d)), tab, tab],
        out_specs=out_specs,
        out_shape=out_shape,
        compiler_params=_params(("arbitrary", "arbitrary")),
        name="dil_in",
    )(x, g, shift, scale, w, qg, kg, cos, sin)


def _dil_attn_kernel(q_ref, kc_ref, kp_ref, vc_ref, vp_ref, o_ref, lse_ref, lse_t_ref):
    w = DIL_WINDOW_KEYS
    tq, d = q_ref.shape[2:]
    n_heads = d // HEAD_DIM
    i = pl.program_id(2)
    lane = lax.broadcasted_iota(jnp.int32, (1, LANES), 1)
    key = lax.broadcasted_iota(jnp.int32, (2 * w, 2 * w), 0)
    qry = lax.broadcasted_iota(jnp.int32, (2 * w, 2 * w), 1) % w
    band = jnp.logical_and(key >= qry, key <= qry + w)
    bias_band = jnp.where(band, 0.0, -jnp.inf)
    in_seq = jnp.logical_or(key >= w, i > 0)
    bias_first = jnp.where(jnp.logical_and(band, in_seq), 0.0, -jnp.inf)
    dim = lax.broadcasted_iota(jnp.int32, (LANES, 1), 0)

    chains = [(p, sub) for p in range(d // LANES) for sub in range(tq // w)]
    zero = jnp.zeros((w, LANES), q_ref.dtype)
    ones_rows = jnp.ones((ONES_ROWS, 2 * w), BF16)

    def window(cur_ref, prev_ref, p, sub):
        cs = slice(p * LANES, (p + 1) * LANES)
        if sub == 0:
            return jnp.concatenate([prev_ref[0, 0, :, cs], cur_ref[0, 0, :w, cs]], axis=0)
        return cur_ref[0, 0, (sub - 1) * w:(sub + 1) * w, cs]

    def logits(c):
        p, sub = chains[c]
        qs = q_ref[0, 0, sub * w:(sub + 1) * w, p * LANES:(p + 1) * LANES]
        q_both = jnp.concatenate([jnp.where(lane < HEAD_DIM, qs, zero),
                                  jnp.where(lane >= HEAD_DIM, qs, zero)], axis=0)
        s = _dot_nt(window(kc_ref, kp_ref, p, sub), q_both)
        return s + (bias_first if sub == 0 else bias_band)

    def finish(c, s):
        p, sub = chains[c]
        rows = slice(sub * w, (sub + 1) * w)
        m = jnp.max(s, axis=0, keepdims=True)
        e = jnp.exp2(s - m)
        v_t = jnp.concatenate([window(vc_ref, vp_ref, p, sub).T, ones_rows], axis=0)
        o_l = _dot(v_t, e.astype(BF16))
        l = o_l[LANES:LANES + 1]
        o_t = o_l[:LANES] * (1.0 / l)
        o_ref[0, 0, rows, p * LANES:(p + 1) * LANES] = (
            jnp.where(dim < HEAD_DIM, o_t[:, :w], o_t[:, w:]).T)
        lse = m + jnp.log(l) * LOG2E
        lse_t_ref[2 * p:2 * p + 1, rows] = lse[:, :w]
        lse_t_ref[2 * p + 1:2 * p + 2, rows] = lse[:, w:]

    lse_t_ref[...] = jnp.zeros_like(lse_t_ref)
    ahead = 4
    pending = [logits(c) for c in range(ahead)]
    for c in range(len(chains)):
        if c + ahead < len(chains):
            pending.append(logits(c + ahead))
        finish(c, pending[c])
    lse_ref[0, 0] = lse_t_ref[...].T


def _dil_attn(q, k, v, tq=256):
    b, dilation, sub_len, d = q.shape
    w = DIL_WINDOW_KEYS
    tq = min(tq, sub_len)
    cur = pl.BlockSpec((1, 1, tq, d), lambda bi, r, i: (bi, r, i, 0))
    prev = pl.BlockSpec((1, 1, w, d),
                        lambda bi, r, i: (bi, r, jnp.maximum(i * (tq // w) - 1, 0), 0))
    return pl.pallas_call(
        _dil_attn_kernel,
        grid=(b, dilation, sub_len // tq),
        in_specs=[cur, cur, prev, cur, prev],
        out_specs=[cur, pl.BlockSpec((1, 1, tq, LANES), lambda bi, r, i: (bi, r, i, 0))],
        out_shape=[jax.ShapeDtypeStruct(q.shape, F32),
                   jax.ShapeDtypeStruct((b, dilation, sub_len, LANES), F32)],
        scratch_shapes=[pltpu.VMEM((LANES, tq), F32)],
        compiler_params=_params(("arbitrary", "arbitrary", "arbitrary")),
        name="dil_attn",
    )(q, k, k, v, v)


def _dil_out_kernel(x_ref, o0_ref, o1_ref, o2_ref, l0_ref, l1_ref, l2_ref, w_ref, gt_ref,
                    out_ref, o_stage_ref, l_stage_ref):
    d = x_ref.shape[2]

    def token_order(ref, stage_ref, slot):
        dilation, n, width = ref.shape[1:]
        if dilation == 1:
            return ref[0, 0]
        n_tiles = width // LANES
        for r in range(dilation):
            for t in range(n_tiles):
                stage_ref[slot * n_tiles + t, pl.ds(r, n, stride=dilation), :] = (
                    ref[0, r, :, t * LANES:(t + 1) * LANES])
        return jnp.concatenate([stage_ref[slot * n_tiles + t] for t in range(n_tiles)], axis=1)

    lses = [token_order(ref, l_stage_ref, g) for g, ref in enumerate((l0_ref, l1_ref, l2_ref))]
    outs = [token_order(ref, o_stage_ref, g) for g, ref in enumerate((o0_ref, o1_ref, o2_ref))]
    m = jnp.maximum(jnp.maximum(lses[0], lses[1]), lses[2])
    es = [jnp.exp2(l - m) for l in lses]
    inv = 1.0 / (es[0] + es[1] + es[2])
    r = lax.broadcasted_iota(jnp.int32, (LANES, d), 0)
    c = lax.broadcasted_iota(jnp.int32, (LANES, d), 1) // HEAD_DIM
    expand = jnp.where(r == c, 1.0, 0.0).astype(BF16)
    merged = jnp.zeros(x_ref.shape[1:], F32)
    for e, o in zip(es, outs):
        alpha = e * inv
        hi = alpha.astype(BF16)
        lo = (alpha - hi.astype(F32)).astype(BF16)
        merged = merged + (_dot(hi, expand) + _dot(lo, expand)) * o
    out_ref[0] = x_ref[0] + gt_ref[0] * _dot(merged.astype(BF16), w_ref[...])


def _dil_out(x, outs, lses, w_out, gate, tm=512):
    b, s, d = x.shape
    act = pl.BlockSpec((1, tm, d), lambda i, j: (i, j, 0))
    grouped = lambda a: pl.BlockSpec((1, a.shape[1], tm // a.shape[1], a.shape[3]),
                                     lambda i, j: (i, 0, j, 0))
    return pl.pallas_call(
        _dil_out_kernel,
        grid=(b, s // tm),
        in_specs=[act] + [grouped(a) for a in outs] + [grouped(a) for a in lses]
        + [_resident(w_out.shape, lambda i, j: (0, 0)),
           pl.BlockSpec((1, 1, d), lambda i, j: (i, 0, 0))],
        out_specs=act,
        out_shape=jax.ShapeDtypeStruct(x.shape, x.dtype),
        scratch_shapes=[pltpu.VMEM((len(outs) * d // LANES, tm, LANES), F32),
                        pltpu.VMEM((len(lses), tm, LANES), F32)],
        compiler_params=_params(("arbitrary", "arbitrary")),
        name="dil_out",
    )(x, *outs, *lses, w_out.astype(BF16), gate)


def kernel(x, c, positions, mod_w, mod_b, norm_g, ffn_w_gate, ffn_w_up, ffn_w_down,
           fox_w_in, fox_b_f, fox_q_g, fox_k_g, fox_w_out,
           dil_w_in, dil_q_g, dil_k_g, dil_w_out):
    b, s, d = x.shape
    depth = mod_w.shape[0]
    mod = _modulation(c, mod_w, mod_b).reshape(depth, b, N_SUBLAYERS, 3, 1, d)
    wg = ffn_w_gate.astype(BF16)
    wu = ffn_w_up.astype(BF16)
    wd = ffn_w_down.astype(BF16)
    cos, sin = _rope_tables(positions)

    for i in range(depth):
        shift = lambda sub: mod[i, :, sub, 0]
        scale = lambda sub: mod[i, :, sub, 1]
        gate = lambda sub: mod[i, :, sub, 2]
        g = lambda sub: norm_g[i, sub].reshape(1, d)

        x = _ffn(x, g(0), shift(0), scale(0), gate(0), wg, wu, wd, i, 0)
        j = i // 2
        if i % 2 == 0:
            q, ka, vt = _fox_in(x, g(1), shift(1), scale(1), fox_w_in[j], fox_b_f[j],
                                fox_q_g[j], fox_k_g[j])
            o = _fox_attn(q, ka, vt)
            x = _out_proj(x, o, fox_w_out[j], gate(1))
        else:
            qkv = _dil_in(x, g(1), shift(1), scale(1), dil_w_in[j], dil_q_g[j], dil_k_g[j],
                          cos, sin)
            outs, lses = [], []
            for grp in range(len(DIL_CONFIGS)):
                o, lse = _dil_attn(qkv[3 * grp], qkv[3 * grp + 1], qkv[3 * grp + 2])
                outs.append(o)
                lses.append(lse)
            x = _dil_out(x, outs, lses, dil_w_out[j], gate(1))
        x = _ffn(x, g(2), shift(2), scale(2), gate(2), wg, wu, wd, i, 1)
    return x
```

```python
import functools

import jax
import jax.numpy as jnp
from jax import lax
from jax.experimental import pallas as pl
from jax.experimental.pallas import tpu as pltpu

HEAD_DIM = 64
ROPE_DIM = 16
ROPE_THETA = 500000.0
EPS = 1e-6
MACARON_WEIGHT = 0.5
N_SUBLAYERS = 3
DIL_CONFIGS = ((128, 1), (512, 4), (2048, 16))
DIL_WINDOW_KEYS = 128

LANES = 128
MXU_DIM = 256
VMEM_LIMIT = 56 * 1024 * 1024

F32 = jnp.float32
BF16 = jnp.bfloat16


def _params(semantics, vmem=VMEM_LIMIT):
    return pltpu.CompilerParams(dimension_semantics=semantics, vmem_limit_bytes=vmem)


def _resident(block_shape, index_map):
    return pl.BlockSpec(block_shape, index_map, pipeline_mode=pl.Buffered(1))


def _dot(a, b):
    return jnp.dot(a, b, preferred_element_type=F32)


def _dot_nt(a, b):
    return lax.dot_general(a, b, (((1,), (1,)), ((), ())), preferred_element_type=F32)


def _sigmoid(x):
    return 1.0 / (1.0 + jnp.exp(-x))


def _norm_mod(x, g, shift, scale):
    ms = jnp.mean(x * x, axis=-1, keepdims=True)
    y = x * lax.rsqrt(ms + EPS) * g
    return y * (1.0 + scale) + shift


def _head_block_ones(n):
    r = lax.broadcasted_iota(jnp.int32, (n, n), 0) // HEAD_DIM
    c = lax.broadcasted_iota(jnp.int32, (n, n), 1) // HEAD_DIM
    return jnp.where(r == c, 1.0, 0.0).astype(BF16)


def _head_rms_norm(x, gain):
    ones = _head_block_ones(MXU_DIM)
    outs = []
    for c in range(x.shape[1] // MXU_DIM):
        xc = x[:, c * MXU_DIM:(c + 1) * MXU_DIM]
        ssq = _dot((xc * xc).astype(BF16), ones)
        r = lax.rsqrt(ssq * (1.0 / HEAD_DIM) + EPS)
        outs.append(xc * r * gain[:, c * MXU_DIM:(c + 1) * MXU_DIM])
    return jnp.concatenate(outs, axis=1)


def _mod_kernel(c_ref, w_ref, b_ref, o_ref):
    c = c_ref[...]
    c_act = c * _sigmoid(c)
    o_ref[0] = jnp.dot(c_act, w_ref[0], preferred_element_type=F32,
                       precision=lax.Precision.HIGHEST) + b_ref[0]


def _modulation(c, mod_w, mod_b):
    depth, d, n = mod_w.shape
    b = c.shape[0]
    tn = 1024
    return pl.pallas_call(
        _mod_kernel,
        grid=(depth, n // tn),
        in_specs=[
            pl.BlockSpec((b, d), lambda i, j: (0, 0)),
            pl.BlockSpec((1, d, tn), lambda i, j: (i, 0, j)),
            pl.BlockSpec((1, 1, tn), lambda i, j: (i, 0, j)),
        ],
        out_specs=pl.BlockSpec((1, b, tn), lambda i, j: (i, 0, j)),
        out_shape=jax.ShapeDtypeStruct((depth, b, n), F32),
        compiler_params=_params(("arbitrary", "arbitrary")),
        name="modulation",
    )(c, mod_w, mod_b.reshape(depth, 1, n))


FFN_CHUNK = 256


def _swiglu_residual(x, g_ref, sh_ref, sc_ref, gt_ref, wg_ref, wu_ref, wd_ref):
    h = _norm_mod(x, g_ref[...], sh_ref[0], sc_ref[0]).astype(BF16)
    acc = jnp.zeros_like(x)
    for c in range(wg_ref.shape[1] // FFN_CHUNK):
        cs = slice(c * FFN_CHUNK, (c + 1) * FFN_CHUNK)
        gg = _dot(h, wg_ref[:, cs])
        uu = _dot(h, wu_ref[:, cs])
        a = (gg * _sigmoid(gg) * uu).astype(BF16)
        acc = acc + _dot(a, wd_ref[cs, :])
    return x + (MACARON_WEIGHT * gt_ref[0]) * acc


def _ffn_kernel(x_ref, *refs):
    ffn_refs, o_ref = refs[:-1], refs[-1]
    o_ref[0] = _swiglu_residual(x_ref[0], *ffn_refs)


def _proj_ffn_kernel(x_ref, a_ref, wo_ref, gm_ref, *refs):
    ffn_refs, o_ref = refs[:-1], refs[-1]
    x = x_ref[0] + gm_ref[0] * _dot(a_ref[0], wo_ref[...])
    o_ref[0] = _swiglu_residual(x, *ffn_refs)


def _dil_ffn_kernel(x_ref, o0_ref, o1_ref, o2_ref, l0_ref, l1_ref, l2_ref, wo_ref, gm_ref, *refs):
    ffn_refs, o_ref, o_stage_ref, l_stage_ref = refs[:-3], refs[-3], refs[-2], refs[-1]
    merged = _dil_merge((o0_ref, o1_ref, o2_ref), (l0_ref, l1_ref, l2_ref),
                        o_stage_ref, l_stage_ref)
    x = x_ref[0] + gm_ref[0] * _dot(merged.astype(BF16), wo_ref[...])
    o_ref[0] = _swiglu_residual(x, *ffn_refs)


def _ffn(x, g, shift, scale, gate, wg, wu, wd, layer, which, mixer=None, tm=512):
    b, s, d = x.shape
    f = wg.shape[-1]
    row = pl.BlockSpec((1, 1, d), lambda i, j: (i, 0, 0))
    act = pl.BlockSpec((1, tm, d), lambda i, j: (i, j, 0))
    ffn_specs = [
        pl.BlockSpec((1, d), lambda i, j: (0, 0)),
        row, row, row,
        _resident((None, None, d, f), lambda i, j: (layer, which, 0, 0)),
        _resident((None, None, d, f), lambda i, j: (layer, which, 0, 0)),
        _resident((None, None, f, d), lambda i, j: (layer, which, 0, 0)),
    ]
    ffn_args = (g, shift, scale, gate, wg, wu, wd)
    scratch = []
    if mixer is None:
        body, mix_specs, mix_args = _ffn_kernel, [], ()
    elif len(mixer) == 3:
        o, w_out, gate_m = mixer
        body = _proj_ffn_kernel
        mix_specs = [act, _resident(w_out.shape, lambda i, j: (0, 0)), row]
        mix_args = (o, w_out.astype(BF16), gate_m)
    else:
        outs, lses, w_out, gate_m = mixer
        body = _dil_ffn_kernel
        grouped = lambda a: pl.BlockSpec((1, a.shape[1], tm // a.shape[1], a.shape[3]),
                                         lambda i, j: (i, 0, j, 0))
        mix_specs = ([grouped(a) for a in outs] + [grouped(a) for a in lses]
                     + [_resident(w_out.shape, lambda i, j: (0, 0)), row])
        mix_args = (*outs, *lses, w_out.astype(BF16), gate_m)
        scratch = [pltpu.VMEM((len(outs) * d // LANES, tm, LANES), F32),
                   pltpu.VMEM((len(lses), tm, LANES), F32)]
    return pl.pallas_call(
        body,
        grid=(b, s // tm),
        in_specs=[act] + mix_specs + ffn_specs,
        out_specs=act,
        out_shape=jax.ShapeDtypeStruct(x.shape, x.dtype),
        scratch_shapes=scratch,
        compiler_params=_params(("arbitrary", "arbitrary")),
        name="ffn",
    )(x, *mix_args, *ffn_args)


def _cumsum_rows(x):
    n = x.shape[0]
    row = lax.broadcasted_iota(jnp.int32, x.shape, 0)
    shift = 1
    while shift < n:
        x = x + jnp.where(row >= shift, pltpu.roll(x, shift, 0), 0.0)
        shift *= 2
    return x


N_SPLIT = 3
LOG2E = 1.4426950408889634


def _fox_in_kernel(x_ref, g_ref, sh_ref, sc_ref, wqk_ref, wvt_ref, wf_ref, bf_ref, qg_ref,
                   kg_ref, sel_ref, q_ref, ka_ref, vt_ref, carry_ref, *, n_heads):
    d = x_ref.shape[2]
    tm = x_ref.shape[1]

    @pl.when(pl.program_id(1) == 0)
    def _():
        carry_ref[...] = jnp.zeros_like(carry_ref)

    h = _norm_mod(x_ref[0], g_ref[...], sh_ref[0], sc_ref[0]).astype(BF16)
    q = _head_rms_norm(_dot(h, wqk_ref[:, 0:d]), qg_ref[...])
    q_ref[0] = (q * (LOG2E * HEAD_DIM ** -0.5)).astype(BF16)
    k = _head_rms_norm(_dot(h, wqk_ref[:, d:2 * d]), kg_ref[...]).astype(BF16)
    vt_ref[0] = _dot_nt(wvt_ref[...], h).astype(BF16)

    z = _dot(h, wf_ref[...]) + bf_ref[...]
    logf = jnp.minimum(z, 0.0) - jnp.log1p(jnp.exp(-jnp.abs(z)))
    lane = lax.broadcasted_iota(jnp.int32, logf.shape, 1)
    logf = jnp.where(lane < n_heads, logf, 0.0)
    cum = _cumsum_rows(logf) + carry_ref[...]
    carry_ref[...] = cum[tm - 1:tm, :]

    terms = []
    rest = cum * LOG2E
    for _ in range(N_SPLIT):
        t = rest.astype(BF16)
        terms.append(t)
        rest = rest - t.astype(F32)
    kx = _dot(jnp.concatenate(terms, axis=1), sel_ref[...]).astype(BF16)
    pieces = []
    for p in range(d // LANES):
        pieces += [k[:, p * LANES:(p + 1) * LANES], kx[:, p * LANES:(p + 1) * LANES]]
    ka_ref[0] = jnp.concatenate(pieces, axis=1)


def _fox_in(x, g, shift, scale, w_in, b_f, q_g, k_g, tm=512):
    b, s, d = x.shape
    n_heads = d // HEAD_DIM
    wqk = w_in[:, :2 * d].astype(BF16)
    wvt = w_in[:, 2 * d:3 * d].T.astype(BF16)
    wf = jnp.pad(w_in[:, 3 * d:], ((0, 0), (0, LANES - n_heads))).astype(BF16)
    bf = jnp.pad(b_f, (0, LANES - n_heads)).reshape(1, LANES)
    qg = jnp.tile(q_g, n_heads).reshape(1, d)
    kg = jnp.tile(k_g, n_heads).reshape(1, d)
    heads = jnp.arange(n_heads)
    sel = jnp.zeros((N_SPLIT * LANES, d), BF16)
    for t in range(N_SPLIT):
        sel = sel.at[t * LANES + heads, (heads // 2) * LANES + (heads % 2) * N_SPLIT + t].set(1.0)
    row = pl.BlockSpec((1, 1, d), lambda i, j: (i, 0, 0))
    full = lambda shp: pl.BlockSpec(shp, lambda i, j: (0, 0))
    res = lambda a: _resident(a.shape, lambda i, j: (0, 0))
    act = pl.BlockSpec((1, tm, d), lambda i, j: (i, j, 0))
    return pl.pallas_call(
        functools.partial(_fox_in_kernel, n_heads=n_heads),
        grid=(b, s // tm),
        in_specs=[act, full((1, d)), row, row, res(wqk), res(wvt), res(wf),
                  full((1, LANES)), full((1, d)), full((1, d)), res(sel)],
        out_specs=[act, pl.BlockSpec((1, tm, 2 * d), lambda i, j: (i, j, 0)),
                   pl.BlockSpec((1, d, tm), lambda i, j: (i, 0, j))],
        out_shape=[jax.ShapeDtypeStruct((b, s, d), BF16),
                   jax.ShapeDtypeStruct((b, s, 2 * d), BF16),
                   jax.ShapeDtypeStruct((b, d, s), BF16)],
        scratch_shapes=[pltpu.VMEM((1, LANES), F32)],
        compiler_params=_params(("arbitrary", "arbitrary")),
        name="fox_in",
    )(x, g, shift, scale, wqk, wvt, wf, bf, qg, kg, sel)


ONES_ROWS = 16
PLAIN_UNROLL = 4


def _fox_attn_kernel(q_ref, ka_ref, vt_ref, o_ref, qa_ref, m_ref, acc_ref, s_ref, *, tq, strip):
    tk = tq
    n_tiles = q_ref.shape[1] // tq
    lane = lax.broadcasted_iota(jnp.int32, (1, LANES), 1)
    q2 = q_ref[0]
    zero = jnp.zeros_like(q2)
    for hh in range(2):
        head = (lane < HEAD_DIM) if hh == 0 else (lane >= HEAD_DIM)
        pick = jnp.logical_and(lane >= hh * N_SPLIT, lane < (hh + 1) * N_SPLIT)
        minus_one = jnp.broadcast_to(jnp.where(pick, -1.0, 0.0), q2.shape).astype(BF16)
        qa_ref[hh] = jnp.concatenate([jnp.where(head, q2, zero), minus_one], axis=1)

    ones_rows = jnp.ones((ONES_ROWS, tk), BF16)
    chains = [(hh, slice(st * strip, (st + 1) * strip), st)
              for hh in range(2) for st in range(tq // strip)]
    n = len(chains)

    def logits(c, i, j, diagonal):
        hh, _, st = chains[c]
        keys = pl.ds(pl.multiple_of(j * tk, tk), tk)
        qrys = pl.ds(pl.multiple_of(i * tq + st * strip, strip), strip)
        s = _dot_nt(ka_ref[0, keys, :], qa_ref[hh, qrys, :])
        if diagonal:
            key = lax.broadcasted_iota(jnp.int32, s.shape, 0)
            qry = lax.broadcasted_iota(jnp.int32, s.shape, 1) + st * strip
            s = jnp.where(key <= qry, s, -jnp.inf)
        return s

    def softmax_pv(c, j):
        hh, cols, _ = chains[c]
        s = s_ref[c]
        m_prev = m_ref[hh, :, cols]
        m_new = jnp.maximum(m_prev, jnp.max(s, axis=0, keepdims=True))
        p = jnp.exp2(s - m_new)
        alpha = jnp.exp2(m_prev - m_new)
        m_ref[hh, :, cols] = m_new
        keys = pl.ds(pl.multiple_of(j * tk, tk), tk)
        vt_ones = jnp.concatenate([vt_ref[0, :, keys], ones_rows], axis=0)
        acc_ref[hh, :, cols] = alpha * acc_ref[hh, :, cols] + _dot(vt_ones, p.astype(BF16))

    def step(j, i_next, j_next, diagonal_next):
        ahead = 2
        s_next = [logits(c, i_next, j_next, diagonal_next) for c in range(ahead)]
        for c in range(n):
            if c + ahead < n:
                s_next.append(logits(c + ahead, i_next, j_next, diagonal_next))
            softmax_pv(c, j)
            s_ref[c] = s_next[c]

    for c in range(n):
        s_ref[c] = logits(c, 0, 0, True)

    def tile(i, carry):
        m_ref[...] = jnp.full_like(m_ref, -jnp.inf)
        acc_ref[...] = jnp.zeros_like(acc_ref)

        n_plain = jnp.maximum(i - 1, 0)

        def plain_steps(j0, count):
            for t in range(count):
                step(j0 + t, i, j0 + t + 1, False)

        def body(t, carry):
            plain_steps(PLAIN_UNROLL * t, PLAIN_UNROLL)
            return carry

        lax.fori_loop(0, n_plain // PLAIN_UNROLL, body, 0)
        done = (n_plain // PLAIN_UNROLL) * PLAIN_UNROLL
        count = PLAIN_UNROLL // 2
        while count >= 1:
            take = (n_plain - done) & count

            @pl.when(take != 0)
            def _(done=done, count=count):
                plain_steps(done, count)

            done = done + take
            count //= 2

        i_next = jnp.minimum(i + 1, n_tiles - 1)

        @pl.when(i > 0)
        def _():
            step(i - 1, i, i, True)
            step(i, i_next, 0, False)

        @pl.when(i == 0)
        def _():
            step(i, i_next, 0, False)

        dim = lax.broadcasted_iota(jnp.int32, (LANES, 1), 0)
        o_heads = [acc_ref[hh, :LANES, :] * (1.0 / acc_ref[hh, LANES:LANES + 1, :])
                   for hh in range(2)]
        rows = pl.ds(pl.multiple_of(i * tq, tq), tq)
        o_ref[0, rows, :] = jnp.where(dim < HEAD_DIM, o_heads[0], o_heads[1]).T.astype(o_ref.dtype)
        return carry

    lax.fori_loop(0, n_tiles, tile, 0)


def _fox_attn(q, ka, vt, tq=512):
    b, s, d = q.shape
    qo = pl.BlockSpec((1, s, LANES), lambda bi, p: (bi, 0, p))
    return pl.pallas_call(
        functools.partial(_fox_attn_kernel, tq=tq, strip=MXU_DIM),
        grid=(b, d // LANES),
        in_specs=[qo, pl.BlockSpec((1, s, 2 * LANES), lambda bi, p: (bi, 0, p)),
                  pl.BlockSpec((1, LANES, s), lambda bi, p: (bi, p, 0))],
        out_specs=qo,
        out_shape=jax.ShapeDtypeStruct((b, s, d), BF16),
        scratch_shapes=[pltpu.VMEM((2, s, 2 * LANES), BF16),
                        pltpu.VMEM((2, 1, tq), F32),
                        pltpu.VMEM((2, LANES + ONES_ROWS, tq), F32),
                        pltpu.VMEM((2 * tq // MXU_DIM, tq, MXU_DIM), F32)],
        compiler_params=_params(("arbitrary", "arbitrary")),
        name="fox_attn",
    )(q, ka, vt)


def _rope_kernel(pos_ref, invf_ref, cos_ref, sin_ref):
    ang = pos_ref[0].astype(F32) * invf_ref[...]
    d = lax.broadcasted_iota(jnp.int32, ang.shape, 1) % HEAD_DIM
    half = ROPE_DIM // 2
    cos_ref[0] = jnp.where(d < ROPE_DIM, jnp.cos(ang), 1.0)
    sn = jnp.sin(ang)
    sin_ref[0] = jnp.where(d < half, -sn, jnp.where(d < ROPE_DIM, sn, 0.0))


def _rope_tables(positions, tm=512):
    b, s = positions.shape
    half = ROPE_DIM // 2
    inv_freq = ROPE_THETA ** (-(jnp.arange(half, dtype=F32) * 2.0 / ROPE_DIM))
    d = jnp.arange(LANES) % HEAD_DIM
    invf = jnp.where(d < ROPE_DIM, inv_freq[d % half], 0.0).reshape(1, LANES)
    tab = pl.BlockSpec((1, tm, LANES), lambda i, j: (i, j, 0))
    return pl.pallas_call(
        _rope_kernel,
        grid=(b, s // tm),
        in_specs=[pl.BlockSpec((1, tm, 1), lambda i, j: (i, j, 0)),
                  pl.BlockSpec((1, LANES), lambda i, j: (0, 0))],
        out_specs=[tab, tab],
        out_shape=[jax.ShapeDtypeStruct((b, s, LANES), F32)] * 2,
        compiler_params=_params(("arbitrary", "arbitrary")),
        name="rope_tables",
    )(positions.reshape(b, s, 1), invf)


def _rope(x, cos, sin):
    half = ROPE_DIM // 2
    d = lax.broadcasted_iota(jnp.int32, (1, LANES), 1) % HEAD_DIM
    outs = []
    for c in range(x.shape[1] // LANES):
        xc = x[:, c * LANES:(c + 1) * LANES]
        partner = jnp.where(d < half, pltpu.roll(xc, LANES - half, 1), pltpu.roll(xc, half, 1))
        outs.append(xc * cos + partner * sin)
    return jnp.concatenate(outs, axis=1)


def _dil_in_kernel(x_ref, g_ref, sh_ref, sc_ref, w_ref, qg_ref, kg_ref, cos_ref, sin_ref,
                   *refs):
    out_refs = refs
    tm, d = x_ref.shape[1:]
    h = _norm_mod(x_ref[0], g_ref[...], sh_ref[0], sc_ref[0]).astype(BF16)
    new = lax.broadcasted_iota(jnp.int32, (tm, tm), 0)
    old = lax.broadcasted_iota(jnp.int32, (tm, tm), 1)
    for grp, (_, dilation) in enumerate(DIL_CONFIGS):
        n = tm // dilation
        if dilation == 1:
            h_g, cos, sin = h, cos_ref[0], sin_ref[0]
        else:
            perm = jnp.where(old == (new % n) * dilation + new // n, 1.0, 0.0).astype(BF16)
            h_g = _dot(perm, h).astype(BF16)
            strided = lambda ref: jnp.concatenate(
                [ref[0, pl.ds(r, n, stride=dilation), :] for r in range(dilation)], axis=0)
            cos, sin = strided(cos_ref), strided(sin_ref)
        base = grp * 3 * d
        q = _head_rms_norm(_dot(h_g, w_ref[:, base:base + d]), qg_ref[grp:grp + 1, :])
        k = _head_rms_norm(_dot(h_g, w_ref[:, base + d:base + 2 * d]), kg_ref[grp:grp + 1, :])
        vals = (_rope(q, cos, sin) * (LOG2E * HEAD_DIM ** -0.5), _rope(k, cos, sin),
                _dot(h_g, w_ref[:, base + 2 * d:base + 3 * d]))
        for out_ref, val in zip(out_refs[3 * grp:3 * grp + 3], vals):
            val = val.astype(BF16)
            for r in range(dilation):
                out_ref[0, r] = val[r * n:(r + 1) * n]


def _dil_in(x, g, shift, scale, w_in, q_g, k_g, cos, sin, tm=256):
    b, s, d = x.shape
    n_groups = len(DIL_CONFIGS)
    n_heads = d // HEAD_DIM
    qg = jnp.tile(q_g, (1, n_heads))
    kg = jnp.tile(k_g, (1, n_heads))
    row = pl.BlockSpec((1, 1, d), lambda i, j: (i, 0, 0))
    full = lambda shp: pl.BlockSpec(shp, lambda i, j: (0, 0))
    act = pl.BlockSpec((1, tm, d), lambda i, j: (i, j, 0))
    tab = pl.BlockSpec((1, tm, LANES), lambda i, j: (i, j, 0))
    w = w_in.astype(BF16)
    out_specs, out_shape = [], []
    for _, dil in DIL_CONFIGS:
        out_specs += [pl.BlockSpec((1, dil, tm // dil, d), lambda i, j: (i, 0, j, 0))] * 3
        out_shape += [jax.ShapeDtypeStruct((b, dil, s // dil, d), BF16)] * 3
    return pl.pallas_call(
        _dil_in_kernel,
        grid=(b, s // tm),
        in_specs=[act, full((1, d)), row, row, _resident(w.shape, lambda i, j: (0, 0)),
                  full((n_groups, d)), full((n_groups, d)), tab, tab],
        out_specs=out_specs,
        out_shape=out_shape,
        compiler_params=_params(("arbitrary", "arbitrary")),
        name="dil_in",
    )(x, g, shift, scale, w, qg, kg, cos, sin)


def _dil_attn_kernel(q_ref, kc_ref, kp_ref, vc_ref, vp_ref, o_ref, lse_ref, lse_t_ref):
    w = DIL_WINDOW_KEYS
    tq, d = q_ref.shape[2:]
    n_heads = d // HEAD_DIM
    i = pl.program_id(2)
    lane = lax.broadcasted_iota(jnp.int32, (1, LANES), 1)
    key = lax.broadcasted_iota(jnp.int32, (2 * w, 2 * w), 0)
    qry = lax.broadcasted_iota(jnp.int32, (2 * w, 2 * w), 1) % w
    band = jnp.logical_and(key >= qry, key <= qry + w)
    bias_band = jnp.where(band, 0.0, -jnp.inf)
    in_seq = jnp.logical_or(key >= w, i > 0)
    bias_first = jnp.where(jnp.logical_and(band, in_seq), 0.0, -jnp.inf)
    dim = lax.broadcasted_iota(jnp.int32, (LANES, 1), 0)

    chains = [(p, sub) for p in range(d // LANES) for sub in range(tq // w)]
    zero = jnp.zeros((w, LANES), q_ref.dtype)
    ones_rows = jnp.ones((ONES_ROWS, 2 * w), BF16)

    def window(cur_ref, prev_ref, p, sub):
        cs = slice(p * LANES, (p + 1) * LANES)
        if sub == 0:
            return jnp.concatenate([prev_ref[0, 0, :, cs], cur_ref[0, 0, :w, cs]], axis=0)
        return cur_ref[0, 0, (sub - 1) * w:(sub + 1) * w, cs]

    def logits(c):
        p, sub = chains[c]
        qs = q_ref[0, 0, sub * w:(sub + 1) * w, p * LANES:(p + 1) * LANES]
        q_both = jnp.concatenate([jnp.where(lane < HEAD_DIM, qs, zero),
                                  jnp.where(lane >= HEAD_DIM, qs, zero)], axis=0)
        s = _dot_nt(window(kc_ref, kp_ref, p, sub), q_both)
        return s + (bias_first if sub == 0 else bias_band)

    def finish(c, s):
        p, sub = chains[c]
        rows = slice(sub * w, (sub + 1) * w)
        m = jnp.max(s, axis=0, keepdims=True)
        e = jnp.exp2(s - m)
        v_t = jnp.concatenate([window(vc_ref, vp_ref, p, sub).T, ones_rows], axis=0)
        o_l = _dot(v_t, e.astype(BF16))
        l = o_l[LANES:LANES + 1]
        o_t = o_l[:LANES] * (1.0 / l)
        o_ref[0, 0, rows, p * LANES:(p + 1) * LANES] = (
            jnp.where(dim < HEAD_DIM, o_t[:, :w], o_t[:, w:]).T)
        lse = m + jnp.log(l) * LOG2E
        lse_t_ref[2 * p:2 * p + 1, rows] = lse[:, :w]
        lse_t_ref[2 * p + 1:2 * p + 2, rows] = lse[:, w:]

    lse_t_ref[...] = jnp.zeros_like(lse_t_ref)
    ahead = 4
    pending = [logits(c) for c in range(ahead)]
    for c in range(len(chains)):
        if c + ahead < len(chains):
            pending.append(logits(c + ahead))
        finish(c, pending[c])
    lse_ref[0, 0] = lse_t_ref[...].T


def _dil_attn(q, k, v, tq=256):
    b, dilation, sub_len, d = q.shape
    w = DIL_WINDOW_KEYS
    tq = min(tq, sub_len)
    cur = pl.BlockSpec((1, 1, tq, d), lambda bi, r, i: (bi, r, i, 0))
    prev = pl.BlockSpec((1, 1, w, d),
                        lambda bi, r, i: (bi, r, jnp.maximum(i * (tq // w) - 1, 0), 0))
    return pl.pallas_call(
        _dil_attn_kernel,
        grid=(b, dilation, sub_len // tq),
        in_specs=[cur, cur, prev, cur, prev],
        out_specs=[cur, pl.BlockSpec((1, 1, tq, LANES), lambda bi, r, i: (bi, r, i, 0))],
        out_shape=[jax.ShapeDtypeStruct(q.shape, F32),
                   jax.ShapeDtypeStruct((b, dilation, sub_len, LANES), F32)],
        scratch_shapes=[pltpu.VMEM((LANES, tq), F32)],
        compiler_params=_params(("arbitrary", "arbitrary", "arbitrary")),
        name="dil_attn",
    )(q, k, k, v, v)


def _dil_merge(o_refs, l_refs, o_stage_ref, l_stage_ref):
    d = o_refs[0].shape[3]

    def token_order(ref, stage_ref, slot):
        dilation, n, width = ref.shape[1:]
        if dilation == 1:
            return ref[0, 0]
        n_tiles = width // LANES
        for r in range(dilation):
            for t in range(n_tiles):
                stage_ref[slot * n_tiles + t, pl.ds(r, n, stride=dilation), :] = (
                    ref[0, r, :, t * LANES:(t + 1) * LANES])
        return jnp.concatenate([stage_ref[slot * n_tiles + t] for t in range(n_tiles)], axis=1)

    lses = [token_order(ref, l_stage_ref, g) for g, ref in enumerate(l_refs)]
    outs = [token_order(ref, o_stage_ref, g) for g, ref in enumerate(o_refs)]
    m = jnp.maximum(jnp.maximum(lses[0], lses[1]), lses[2])
    es = [jnp.exp2(l - m) for l in lses]
    inv = 1.0 / (es[0] + es[1] + es[2])
    r = lax.broadcasted_iota(jnp.int32, (LANES, d), 0)
    c = lax.broadcasted_iota(jnp.int32, (LANES, d), 1) // HEAD_DIM
    expand = jnp.where(r == c, 1.0, 0.0).astype(BF16)
    merged = outs[-1]
    for e, o in zip(es[:-1], outs[:-1]):
        alpha = e * inv
        hi = alpha.astype(BF16)
        lo = (alpha - hi.astype(F32)).astype(BF16)
        merged = merged + (_dot(hi, expand) + _dot(lo, expand)) * (o - outs[-1])
    return merged


def kernel(x, c, positions, mod_w, mod_b, norm_g, ffn_w_gate, ffn_w_up, ffn_w_down,
           fox_w_in, fox_b_f, fox_q_g, fox_k_g, fox_w_out,
           dil_w_in, dil_q_g, dil_k_g, dil_w_out):
    b, s, d = x.shape
    depth = mod_w.shape[0]
    mod = _modulation(c, mod_w, mod_b).reshape(depth, b, N_SUBLAYERS, 3, 1, d)
    wg = ffn_w_gate.astype(BF16)
    wu = ffn_w_up.astype(BF16)
    wd = ffn_w_down.astype(BF16)
    cos, sin = _rope_tables(positions)

    for i in range(depth):
        shift = lambda sub: mod[i, :, sub, 0]
        scale = lambda sub: mod[i, :, sub, 1]
        gate = lambda sub: mod[i, :, sub, 2]
        g = lambda sub: norm_g[i, sub].reshape(1, d)

        x = _ffn(x, g(0), shift(0), scale(0), gate(0), wg, wu, wd, i, 0)
        j = i // 2
        if i % 2 == 0:
            q, ka, vt = _fox_in(x, g(1), shift(1), scale(1), fox_w_in[j], fox_b_f[j],
                                fox_q_g[j], fox_k_g[j])
            mixer = (_fox_attn(q, ka, vt), fox_w_out[j], gate(1))
        else:
            qkv = _dil_in(x, g(1), shift(1), scale(1), dil_w_in[j], dil_q_g[j], dil_k_g[j],
                          cos, sin)
            outs, lses = [], []
            for grp in range(len(DIL_CONFIGS)):
                o, lse = _dil_attn(qkv[3 * grp], qkv[3 * grp + 1], qkv[3 * grp + 2])
                outs.append(o)
                lses.append(lse)
            mixer = (outs, lses, dil_w_out[j], gate(1))
        x = _ffn(x, g(2), shift(2), scale(2), gate(2), wg, wu, wd, i, 1, mixer=mixer)
    return x
```

```python
import functools

import jax
import jax.numpy as jnp
from jax import lax
from jax.experimental import pallas as pl
from jax.experimental.pallas import tpu as pltpu

HEAD_DIM = 64
ROPE_DIM = 16
ROPE_THETA = 500000.0
EPS = 1e-6
MACARON_WEIGHT = 0.5
N_SUBLAYERS = 3
DIL_CONFIGS = ((128, 1), (512, 4), (2048, 16))
DIL_WINDOW_KEYS = 128

LANES = 128
MXU_DIM = 256
VMEM_LIMIT = 56 * 1024 * 1024

F32 = jnp.float32
BF16 = jnp.bfloat16


def _params(semantics, vmem=VMEM_LIMIT):
    return pltpu.CompilerParams(dimension_semantics=semantics, vmem_limit_bytes=vmem)


def _resident(block_shape, index_map):
    return pl.BlockSpec(block_shape, index_map, pipeline_mode=pl.Buffered(1))


def _dot(a, b):
    return jnp.dot(a, b, preferred_element_type=F32)


def _dot_nt(a, b):
    return lax.dot_general(a, b, (((1,), (1,)), ((), ())), preferred_element_type=F32)


def _sigmoid(x):
    return 1.0 / (1.0 + jnp.exp(-x))


def _norm_mod(x, g, shift, scale):
    ms = jnp.mean(x * x, axis=-1, keepdims=True)
    y = x * lax.rsqrt(ms + EPS) * g
    return y * (1.0 + scale) + shift


def _head_block_ones(n):
    r = lax.broadcasted_iota(jnp.int32, (n, n), 0) // HEAD_DIM
    c = lax.broadcasted_iota(jnp.int32, (n, n), 1) // HEAD_DIM
    return jnp.where(r == c, 1.0, 0.0).astype(BF16)


def _head_rms_norm(x, gain):
    ones = _head_block_ones(MXU_DIM)
    outs = []
    for c in range(x.shape[1] // MXU_DIM):
        xc = x[:, c * MXU_DIM:(c + 1) * MXU_DIM]
        ssq = _dot((xc * xc).astype(BF16), ones)
        r = lax.rsqrt(ssq * (1.0 / HEAD_DIM) + EPS)
        outs.append(xc * r * gain[:, c * MXU_DIM:(c + 1) * MXU_DIM])
    return jnp.concatenate(outs, axis=1)


def _mod_kernel(c_ref, w_ref, b_ref, o_ref):
    c = c_ref[...]
    c_act = c * _sigmoid(c)
    o_ref[0] = jnp.dot(c_act, w_ref[0], preferred_element_type=F32,
                       precision=lax.Precision.HIGHEST) + b_ref[0]


def _modulation(c, mod_w, mod_b):
    depth, d, n = mod_w.shape
    b = c.shape[0]
    tn = 1024
    return pl.pallas_call(
        _mod_kernel,
        grid=(depth, n // tn),
        in_specs=[
            pl.BlockSpec((b, d), lambda i, j: (0, 0)),
            pl.BlockSpec((1, d, tn), lambda i, j: (i, 0, j)),
            pl.BlockSpec((1, 1, tn), lambda i, j: (i, 0, j)),
        ],
        out_specs=pl.BlockSpec((1, b, tn), lambda i, j: (i, 0, j)),
        out_shape=jax.ShapeDtypeStruct((depth, b, n), F32),
        compiler_params=_params(("arbitrary", "arbitrary")),
        name="modulation",
    )(c, mod_w, mod_b.reshape(depth, 1, n))


FFN_CHUNK = 256


def _swiglu_residual(x, g_ref, sh_ref, sc_ref, gt_ref, wg_ref, wu_ref, wd_ref):
    h = _norm_mod(x, g_ref[...], sh_ref[0], sc_ref[0]).astype(BF16)
    acc = jnp.zeros_like(x)
    for c in range(wg_ref.shape[1] // FFN_CHUNK):
        cs = slice(c * FFN_CHUNK, (c + 1) * FFN_CHUNK)
        gg = _dot(h, wg_ref[:, cs])
        uu = _dot(h, wu_ref[:, cs])
        a = (gg * _sigmoid(gg) * uu).astype(BF16)
        acc = acc + _dot(a, wd_ref[cs, :])
    return x + (MACARON_WEIGHT * gt_ref[0]) * acc


def _ffn_kernel(x_ref, *refs):
    ffn_refs, o_ref = refs[:-1], refs[-1]
    o_ref[0] = _swiglu_residual(x_ref[0], *ffn_refs)


def _proj_ffn_kernel(x_ref, a_ref, wo_ref, gm_ref, *refs):
    ffn_refs, o_ref = refs[:-1], refs[-1]
    x = x_ref[0] + gm_ref[0] * _dot(a_ref[0], wo_ref[...])
    o_ref[0] = _swiglu_residual(x, *ffn_refs)


def _dil_ffn_kernel(x_ref, o0_ref, o1_ref, o2_ref, l0_ref, l1_ref, l2_ref, wo_ref, gm_ref, *refs):
    ffn_refs, o_ref, o_stage_ref, l_stage_ref = refs[:-3], refs[-3], refs[-2], refs[-1]
    merged = _dil_merge((o0_ref, o1_ref, o2_ref), (l0_ref, l1_ref, l2_ref),
                        o_stage_ref, l_stage_ref)
    x = x_ref[0] + gm_ref[0] * _dot(merged.astype(BF16), wo_ref[...])
    o_ref[0] = _swiglu_residual(x, *ffn_refs)


def _ffn(x, g, shift, scale, gate, wg, wu, wd, layer, which, mixer=None, tm=512):
    b, s, d = x.shape
    f = wg.shape[-1]
    row = pl.BlockSpec((1, 1, d), lambda i, j: (i, 0, 0))
    act = pl.BlockSpec((1, tm, d), lambda i, j: (i, j, 0))
    ffn_specs = [
        pl.BlockSpec((1, d), lambda i, j: (0, 0)),
        row, row, row,
        _resident((None, None, d, f), lambda i, j: (layer, which, 0, 0)),
        _resident((None, None, d, f), lambda i, j: (layer, which, 0, 0)),
        _resident((None, None, f, d), lambda i, j: (layer, which, 0, 0)),
    ]
    ffn_args = (g, shift, scale, gate, wg, wu, wd)
    scratch = []
    if mixer is None:
        body, mix_specs, mix_args = _ffn_kernel, [], ()
    elif len(mixer) == 3:
        o, w_out, gate_m = mixer
        body = _proj_ffn_kernel
        mix_specs = [act, _resident(w_out.shape, lambda i, j: (0, 0)), row]
        mix_args = (o, w_out.astype(BF16), gate_m)
    else:
        outs, lses, w_out, gate_m = mixer
        body = _dil_ffn_kernel
        grouped = lambda a: pl.BlockSpec((1, a.shape[1], tm // a.shape[1], a.shape[3]),
                                         lambda i, j: (i, 0, j, 0))
        mix_specs = ([grouped(a) for a in outs] + [grouped(a) for a in lses]
                     + [_resident(w_out.shape, lambda i, j: (0, 0)), row])
        mix_args = (*outs, *lses, w_out.astype(BF16), gate_m)
        scratch = [pltpu.VMEM((len(outs) * d // LANES, tm, LANES), F32),
                   pltpu.VMEM((len(lses), tm, LANES), F32)]
    return pl.pallas_call(
        body,
        grid=(b, s // tm),
        in_specs=[act] + mix_specs + ffn_specs,
        out_specs=act,
        out_shape=jax.ShapeDtypeStruct(x.shape, x.dtype),
        scratch_shapes=scratch,
        compiler_params=_params(("arbitrary", "arbitrary")),
        name="ffn",
    )(x, *mix_args, *ffn_args)


def _cumsum_rows(x):
    n = x.shape[0]
    row = lax.broadcasted_iota(jnp.int32, x.shape, 0)
    shift = 1
    while shift < n:
        x = x + jnp.where(row >= shift, pltpu.roll(x, shift, 0), 0.0)
        shift *= 2
    return x


N_SPLIT = 3
LOG2E = 1.4426950408889634


def _fox_in_kernel(x_ref, g_ref, sh_ref, sc_ref, wqk_ref, wvt_ref, wf_ref, bf_ref, qg_ref,
                   kg_ref, sel_ref, q_ref, ka_ref, vt_ref, carry_ref, *, n_heads):
    d = x_ref.shape[2]
    tm = x_ref.shape[1]

    @pl.when(pl.program_id(1) == 0)
    def _():
        carry_ref[...] = jnp.zeros_like(carry_ref)

    h = _norm_mod(x_ref[0], g_ref[...], sh_ref[0], sc_ref[0]).astype(BF16)
    q = _head_rms_norm(_dot(h, wqk_ref[:, 0:d]), qg_ref[...])
    q_ref[0] = (q * (LOG2E * HEAD_DIM ** -0.5)).astype(BF16)
    k = _head_rms_norm(_dot(h, wqk_ref[:, d:2 * d]), kg_ref[...]).astype(BF16)
    vt_ref[0] = _dot_nt(wvt_ref[...], h).astype(BF16)

    z = _dot(h, wf_ref[...]) + bf_ref[...]
    logf = jnp.minimum(z, 0.0) - jnp.log1p(jnp.exp(-jnp.abs(z)))
    lane = lax.broadcasted_iota(jnp.int32, logf.shape, 1)
    logf = jnp.where(lane < n_heads, logf, 0.0)
    cum = _cumsum_rows(logf) + carry_ref[...]
    carry_ref[...] = cum[tm - 1:tm, :]

    terms = []
    rest = cum * LOG2E
    for _ in range(N_SPLIT):
        t = rest.astype(BF16)
        terms.append(t)
        rest = rest - t.astype(F32)
    kx = _dot(jnp.concatenate(terms, axis=1), sel_ref[...]).astype(BF16)
    pieces = []
    for p in range(d // LANES):
        pieces += [k[:, p * LANES:(p + 1) * LANES], kx[:, p * LANES:(p + 1) * LANES]]
    ka_ref[0] = jnp.concatenate(pieces, axis=1)


def _fox_in(x, g, shift, scale, w_in, b_f, q_g, k_g, tm=512):
    b, s, d = x.shape
    n_heads = d // HEAD_DIM
    wqk = w_in[:, :2 * d].astype(BF16)
    wvt = w_in[:, 2 * d:3 * d].T.astype(BF16)
    wf = jnp.pad(w_in[:, 3 * d:], ((0, 0), (0, LANES - n_heads))).astype(BF16)
    bf = jnp.pad(b_f, (0, LANES - n_heads)).reshape(1, LANES)
    qg = jnp.tile(q_g, n_heads).reshape(1, d)
    kg = jnp.tile(k_g, n_heads).reshape(1, d)
    heads = jnp.arange(n_heads)
    sel = jnp.zeros((N_SPLIT * LANES, d), BF16)
    for t in range(N_SPLIT):
        sel = sel.at[t * LANES + heads, (heads // 2) * LANES + (heads % 2) * N_SPLIT + t].set(1.0)
    row = pl.BlockSpec((1, 1, d), lambda i, j: (i, 0, 0))
    full = lambda shp: pl.BlockSpec(shp, lambda i, j: (0, 0))
    res = lambda a: _resident(a.shape, lambda i, j: (0, 0))
    act = pl.BlockSpec((1, tm, d), lambda i, j: (i, j, 0))
    return pl.pallas_call(
        functools.partial(_fox_in_kernel, n_heads=n_heads),
        grid=(b, s // tm),
        in_specs=[act, full((1, d)), row, row, res(wqk), res(wvt), res(wf),
                  full((1, LANES)), full((1, d)), full((1, d)), res(sel)],
        out_specs=[act, pl.BlockSpec((1, tm, 2 * d), lambda i, j: (i, j, 0)),
                   pl.BlockSpec((1, d, tm), lambda i, j: (i, 0, j))],
        out_shape=[jax.ShapeDtypeStruct((b, s, d), BF16),
                   jax.ShapeDtypeStruct((b, s, 2 * d), BF16),
                   jax.ShapeDtypeStruct((b, d, s), BF16)],
        scratch_shapes=[pltpu.VMEM((1, LANES), F32)],
        compiler_params=_params(("arbitrary", "arbitrary")),
        name="fox_in",
    )(x, g, shift, scale, wqk, wvt, wf, bf, qg, kg, sel)


ONES_ROWS = 16
STEPS_PER_BLOCK = 4


def _fox_attn_kernel(q_ref, ka_ref, vt_ref, o_ref, qa_ref, m_ref, acc_ref, s_ref, *, tq, strip):
    tk = tq
    n_tiles = q_ref.shape[1] // tq
    lane = lax.broadcasted_iota(jnp.int32, (1, LANES), 1)
    q2 = q_ref[0]
    zero = jnp.zeros_like(q2)
    for hh in range(2):
        head = (lane < HEAD_DIM) if hh == 0 else (lane >= HEAD_DIM)
        pick = jnp.logical_and(lane >= hh * N_SPLIT, lane < (hh + 1) * N_SPLIT)
        minus_one = jnp.broadcast_to(jnp.where(pick, -1.0, 0.0), q2.shape).astype(BF16)
        qa_ref[hh] = jnp.concatenate([jnp.where(head, q2, zero), minus_one], axis=1)

    ones_rows = jnp.ones((ONES_ROWS, tk), BF16)
    chains = [(hh, slice(st * strip, (st + 1) * strip), st)
              for hh in range(2) for st in range(tq // strip)]
    n = len(chains)

    def logits(c, i, j, diagonal):
        hh, _, st = chains[c]
        keys = slice(j * tk, (j + 1) * tk)
        qrys = slice(i * tq + st * strip, i * tq + (st + 1) * strip)
        s = _dot_nt(ka_ref[0, keys, :], qa_ref[hh, qrys, :])
        if diagonal:
            key = lax.broadcasted_iota(jnp.int32, s.shape, 0)
            qry = lax.broadcasted_iota(jnp.int32, s.shape, 1) + st * strip
            s = jnp.where(key <= qry, s, -jnp.inf)
        return s

    def softmax_pv(c, j):
        hh, cols, _ = chains[c]
        s = s_ref[c]
        m_prev = m_ref[hh, :, cols]
        m_new = jnp.maximum(m_prev, jnp.max(s, axis=0, keepdims=True))
        p = jnp.exp2(s - m_new)
        alpha = jnp.exp2(m_prev - m_new)
        m_ref[hh, :, cols] = m_new
        vt_ones = jnp.concatenate([vt_ref[0, :, j * tk:(j + 1) * tk], ones_rows], axis=0)
        acc_ref[hh, :, cols] = alpha * acc_ref[hh, :, cols] + _dot(vt_ones, p.astype(BF16))

    def step(i, j, nxt):
        if j == 0:
            m_ref[...] = jnp.full_like(m_ref, -jnp.inf)
            acc_ref[...] = jnp.zeros_like(acc_ref)
        ahead = 2
        s_next = []
        for c in range(n + ahead):
            if nxt is not None and c < n:
                s_next.append(logits(c, nxt[0], nxt[1], nxt[0] == nxt[1]))
            if c >= ahead:
                softmax_pv(c - ahead, j)
                if nxt is not None:
                    s_ref[c - ahead] = s_next[c - ahead]
        if j == i:
            dim = lax.broadcasted_iota(jnp.int32, (LANES, 1), 0)
            o_heads = [acc_ref[hh, :LANES, :] * (1.0 / acc_ref[hh, LANES:LANES + 1, :])
                       for hh in range(2)]
            o_ref[0, i * tq:(i + 1) * tq, :] = (
                jnp.where(dim < HEAD_DIM, o_heads[0], o_heads[1]).T.astype(o_ref.dtype))

    for c in range(n):
        s_ref[c] = logits(c, 0, 0, True)

    sequence = [(i, j) for i in range(n_tiles) for j in range(i + 1)]
    always = pl.program_id(0) >= 0
    for start in range(0, len(sequence), STEPS_PER_BLOCK):

        @pl.when(always)
        def _(start=start):
            for idx in range(start, min(start + STEPS_PER_BLOCK, len(sequence))):
                nxt = sequence[idx + 1] if idx + 1 < len(sequence) else None
                step(*sequence[idx], nxt)


def _fox_attn(q, ka, vt, tq=512):
    b, s, d = q.shape
    qo = pl.BlockSpec((1, s, LANES), lambda bi, p: (bi, 0, p))
    return pl.pallas_call(
        functools.partial(_fox_attn_kernel, tq=tq, strip=MXU_DIM),
        grid=(b, d // LANES),
        in_specs=[qo, pl.BlockSpec((1, s, 2 * LANES), lambda bi, p: (bi, 0, p)),
                  pl.BlockSpec((1, LANES, s), lambda bi, p: (bi, p, 0))],
        out_specs=qo,
        out_shape=jax.ShapeDtypeStruct((b, s, d), BF16),
        scratch_shapes=[pltpu.VMEM((2, s, 2 * LANES), BF16),
                        pltpu.VMEM((2, 1, tq), F32),
                        pltpu.VMEM((2, LANES + ONES_ROWS, tq), F32),
                        pltpu.VMEM((2 * tq // MXU_DIM, tq, MXU_DIM), F32)],
        compiler_params=_params(("arbitrary", "arbitrary")),
        name="fox_attn",
    )(q, ka, vt)


def _rope_kernel(pos_ref, invf_ref, cos_ref, sin_ref):
    ang = pos_ref[0].astype(F32) * invf_ref[...]
    d = lax.broadcasted_iota(jnp.int32, ang.shape, 1) % HEAD_DIM
    half = ROPE_DIM // 2
    cos_ref[0] = jnp.where(d < ROPE_DIM, jnp.cos(ang), 1.0)
    sn = jnp.sin(ang)
    sin_ref[0] = jnp.where(d < half, -sn, jnp.where(d < ROPE_DIM, sn, 0.0))


def _rope_tables(positions, tm=512):
    b, s = positions.shape
    half = ROPE_DIM // 2
    inv_freq = ROPE_THETA ** (-(jnp.arange(half, dtype=F32) * 2.0 / ROPE_DIM))
    d = jnp.arange(LANES) % HEAD_DIM
    invf = jnp.where(d < ROPE_DIM, inv_freq[d % half], 0.0).reshape(1, LANES)
    tab = pl.BlockSpec((1, tm, LANES), lambda i, j: (i, j, 0))
    return pl.pallas_call(
        _rope_kernel,
        grid=(b, s // tm),
        in_specs=[pl.BlockSpec((1, tm, 1), lambda i, j: (i, j, 0)),
                  pl.BlockSpec((1, LANES), lambda i, j: (0, 0))],
        out_specs=[tab, tab],
        out_shape=[jax.ShapeDtypeStruct((b, s, LANES), F32)] * 2,
        compiler_params=_params(("arbitrary", "arbitrary")),
        name="rope_tables",
    )(positions.reshape(b, s, 1), invf)


def _rope(x, cos, sin):
    half = ROPE_DIM // 2
    d = lax.broadcasted_iota(jnp.int32, (1, LANES), 1) % HEAD_DIM
    outs = []
    for c in range(x.shape[1] // LANES):
        xc = x[:, c * LANES:(c + 1) * LANES]
        partner = jnp.where(d < half, pltpu.roll(xc, LANES - half, 1), pltpu.roll(xc, half, 1))
        outs.append(xc * cos + partner * sin)
    return jnp.concatenate(outs, axis=1)


def _dil_in_kernel(x_ref, g_ref, sh_ref, sc_ref, w_ref, qg_ref, kg_ref, cos_ref, sin_ref,
                   *refs):
    out_refs = refs
    tm, d = x_ref.shape[1:]
    h = _norm_mod(x_ref[0], g_ref[...], sh_ref[0], sc_ref[0]).astype(BF16)
    new = lax.broadcasted_iota(jnp.int32, (tm, tm), 0)
    old = lax.broadcasted_iota(jnp.int32, (tm, tm), 1)
    for grp, (_, dilation) in enumerate(DIL_CONFIGS):
        n = tm // dilation
        if dilation == 1:
            h_g, cos, sin = h, cos_ref[0], sin_ref[0]
        else:
            perm = jnp.where(old == (new % n) * dilation + new // n, 1.0, 0.0).astype(BF16)
            h_g = _dot(perm, h).astype(BF16)
            strided = lambda ref: jnp.concatenate(
                [ref[0, pl.ds(r, n, stride=dilation), :] for r in range(dilation)], axis=0)
            cos, sin = strided(cos_ref), strided(sin_ref)
        base = grp * 3 * d
        q = _head_rms_norm(_dot(h_g, w_ref[:, base:base + d]), qg_ref[grp:grp + 1, :])
        k = _head_rms_norm(_dot(h_g, w_ref[:, base + d:base + 2 * d]), kg_ref[grp:grp + 1, :])
        vals = (_rope(q, cos, sin) * (LOG2E * HEAD_DIM ** -0.5), _rope(k, cos, sin),
                _dot(h_g, w_ref[:, base + 2 * d:base + 3 * d]))
        for out_ref, val in zip(out_refs[3 * grp:3 * grp + 3], vals):
            val = val.astype(BF16)
            for r in range(dilation):
                out_ref[0, r] = val[r * n:(r + 1) * n]


def _dil_in(x, g, shift, scale, w_in, q_g, k_g, cos, sin, tm=256):
    b, s, d = x.shape
    n_groups = len(DIL_CONFIGS)
    n_heads = d // HEAD_DIM
    qg = jnp.tile(q_g, (1, n_heads))
    kg = jnp.tile(k_g, (1, n_heads))
    row = pl.BlockSpec((1, 1, d), lambda i, j: (i, 0, 0))
    full = lambda shp: pl.BlockSpec(shp, lambda i, j: (0, 0))
    act = pl.BlockSpec((1, tm, d), lambda i, j: (i, j, 0))
    tab = pl.BlockSpec((1, tm, LANES), lambda i, j: (i, j, 0))
    w = w_in.astype(BF16)
    out_specs, out_shape = [], []
    for _, dil in DIL_CONFIGS:
        out_specs += [pl.BlockSpec((1, dil, tm // dil, d), lambda i, j: (i, 0, j, 0))] * 3
        out_shape += [jax.ShapeDtypeStruct((b, dil, s // dil, d), BF16)] * 3
    return pl.pallas_call(
        _dil_in_kernel,
        grid=(b, s // tm),
        in_specs=[act, full((1, d)), row, row, _resident(w.shape, lambda i, j: (0, 0)),
                  full((n_groups, d)), full((n_groups, d)), tab, tab],
        out_specs=out_specs,
        out_shape=out_shape,
        compiler_params=_params(("arbitrary", "arbitrary")),
        name="dil_in",
    )(x, g, shift, scale, w, qg, kg, cos, sin)


def _dil_attn_kernel(q_ref, kc_ref, kp_ref, vc_ref, vp_ref, o_ref, lse_ref, lse_t_ref):
    w = DIL_WINDOW_KEYS
    n_res, tq, d = q_ref.shape[1:]
    i = pl.program_id(2)
    lane = lax.broadcasted_iota(jnp.int32, (1, LANES), 1)
    key = lax.broadcasted_iota(jnp.int32, (2 * w, 2 * w), 0)
    qry = lax.broadcasted_iota(jnp.int32, (2 * w, 2 * w), 1) % w
    band = jnp.logical_and(key >= qry, key <= qry + w)
    bias_band = jnp.where(band, 0.0, -jnp.inf)
    in_seq = jnp.logical_or(key >= w, i > 0)
    bias_first = jnp.where(jnp.logical_and(band, in_seq), 0.0, -jnp.inf)
    dim = lax.broadcasted_iota(jnp.int32, (LANES, 1), 0)

    chains = [(res, p, sub) for res in range(n_res) for p in range(d // LANES)
              for sub in range(tq // w)]
    zero = jnp.zeros((w, LANES), q_ref.dtype)
    ones_rows = jnp.ones((ONES_ROWS, 2 * w), BF16)

    def window(cur_ref, prev_ref, res, p, sub):
        cs = slice(p * LANES, (p + 1) * LANES)
        if sub == 0:
            return jnp.concatenate([prev_ref[0, res, :, cs], cur_ref[0, res, :w, cs]], axis=0)
        return cur_ref[0, res, (sub - 1) * w:(sub + 1) * w, cs]

    def logits(c):
        res, p, sub = chains[c]
        qs = q_ref[0, res, sub * w:(sub + 1) * w, p * LANES:(p + 1) * LANES]
        q_both = jnp.concatenate([jnp.where(lane < HEAD_DIM, qs, zero),
                                  jnp.where(lane >= HEAD_DIM, qs, zero)], axis=0)
        s = _dot_nt(window(kc_ref, kp_ref, res, p, sub), q_both)
        return s + (bias_first if sub == 0 else bias_band)

    def finish(c, s):
        res, p, sub = chains[c]
        rows = slice(sub * w, (sub + 1) * w)
        m = jnp.max(s, axis=0, keepdims=True)
        e = jnp.exp2(s - m)
        v_t = jnp.concatenate([window(vc_ref, vp_ref, res, p, sub).T, ones_rows], axis=0)
        o_l = _dot(v_t, e.astype(BF16))
        l = o_l[LANES:LANES + 1]
        o_t = o_l[:LANES] * (1.0 / l)
        o_ref[0, res, rows, p * LANES:(p + 1) * LANES] = (
            jnp.where(dim < HEAD_DIM, o_t[:, :w], o_t[:, w:]).T)
        lse = m + jnp.log(l) * LOG2E
        lse_t_ref[res, 2 * p:2 * p + 1, rows] = lse[:, :w]
        lse_t_ref[res, 2 * p + 1:2 * p + 2, rows] = lse[:, w:]

    lse_t_ref[...] = jnp.zeros_like(lse_t_ref)
    ahead = 4
    pending = [logits(c) for c in range(ahead)]
    for c in range(len(chains)):
        if c + ahead < len(chains):
            pending.append(logits(c + ahead))
        finish(c, pending[c])
    for res in range(n_res):
        lse_ref[0, res] = lse_t_ref[res].T


def _dil_attn(q, k, v, rows_per_step=512):
    b, dilation, sub_len, d = q.shape
    w = DIL_WINDOW_KEYS
    tq = min(rows_per_step, sub_len)
    n_res = min(rows_per_step // tq, dilation)
    cur = pl.BlockSpec((1, n_res, tq, d), lambda bi, r, i: (bi, r, i, 0))
    prev = pl.BlockSpec((1, n_res, w, d),
                        lambda bi, r, i: (bi, r, jnp.maximum(i * (tq // w) - 1, 0), 0))
    return pl.pallas_call(
        _dil_attn_kernel,
        grid=(b, dilation // n_res, sub_len // tq),
        in_specs=[cur, cur, prev, cur, prev],
        out_specs=[cur, pl.BlockSpec((1, n_res, tq, LANES), lambda bi, r, i: (bi, r, i, 0))],
        out_shape=[jax.ShapeDtypeStruct(q.shape, F32),
                   jax.ShapeDtypeStruct((b, dilation, sub_len, LANES), F32)],
        scratch_shapes=[pltpu.VMEM((n_res, LANES, tq), F32)],
        compiler_params=_params(("arbitrary", "arbitrary", "arbitrary")),
        name="dil_attn",
    )(q, k, k, v, v)


def _dil_merge(o_refs, l_refs, o_stage_ref, l_stage_ref):
    d = o_refs[0].shape[3]

    def token_order(ref, stage_ref, slot):
        dilation, n, width = ref.shape[1:]
        if dilation == 1:
            return ref[0, 0]
        n_tiles = width // LANES
        for r in range(dilation):
            for t in range(n_tiles):
                stage_ref[slot * n_tiles + t, pl.ds(r, n, stride=dilation), :] = (
                    ref[0, r, :, t * LANES:(t + 1) * LANES])
        return jnp.concatenate([stage_ref[slot * n_tiles + t] for t in range(n_tiles)], axis=1)

    lses = [token_order(ref, l_stage_ref, g) for g, ref in enumerate(l_refs)]
    outs = [token_order(ref, o_stage_ref, g) for g, ref in enumerate(o_refs)]
    m = jnp.maximum(jnp.maximum(lses[0], lses[1]), lses[2])
    es = [jnp.exp2(l - m) for l in lses]
    inv = 1.0 / (es[0] + es[1] + es[2])
    r = lax.broadcasted_iota(jnp.int32, (LANES, d), 0)
    c = lax.broadcasted_iota(jnp.int32, (LANES, d), 1) // HEAD_DIM
    expand = jnp.where(r == c, 1.0, 0.0).astype(BF16)
    merged = outs[-1]
    for e, o in zip(es[:-1], outs[:-1]):
        alpha = e * inv
        hi = alpha.astype(BF16)
        lo = (alpha - hi.astype(F32)).astype(BF16)
        merged = merged + (_dot(hi, expand) + _dot(lo, expand)) * (o - outs[-1])
    return merged


def kernel(x, c, positions, mod_w, mod_b, norm_g, ffn_w_gate, ffn_w_up, ffn_w_down,
           fox_w_in, fox_b_f, fox_q_g, fox_k_g, fox_w_out,
           dil_w_in, dil_q_g, dil_k_g, dil_w_out):
    b, s, d = x.shape
    depth = mod_w.shape[0]
    mod = _modulation(c, mod_w, mod_b).reshape(depth, b, N_SUBLAYERS, 3, 1, d)
    wg = ffn_w_gate.astype(BF16)
    wu = ffn_w_up.astype(BF16)
    wd = ffn_w_down.astype(BF16)
    cos, sin = _rope_tables(positions)

    for i in range(depth):
        shift = lambda sub: mod[i, :, sub, 0]
        scale = lambda sub: mod[i, :, sub, 1]
        gate = lambda sub: mod[i, :, sub, 2]
        g = lambda sub: norm_g[i, sub].reshape(1, d)

        x = _ffn(x, g(0), shift(0), scale(0), gate(0), wg, wu, wd, i, 0)
        j = i // 2
        if i % 2 == 0:
            q, ka, vt = _fox_in(x, g(1), shift(1), scale(1), fox_w_in[j], fox_b_f[j],
                                fox_q_g[j], fox_k_g[j])
            mixer = (_fox_attn(q, ka, vt), fox_w_out[j], gate(1))
        else:
            qkv = _dil_in(x, g(1), shift(1), scale(1), dil_w_in[j], dil_q_g[j], dil_k_g[j],
                          cos, sin)
            outs, lses = [], []
            for grp in range(len(DIL_CONFIGS)):
                o, lse = _dil_attn(qkv[3 * grp], qkv[3 * grp + 1], qkv[3 * grp + 2])
                outs.append(o)
                lses.append(lse)
            mixer = (outs, lses, dil_w_out[j], gate(1))
        x = _ffn(x, g(2), shift(2), scale(2), gate(2), wg, wu, wd, i, 1, mixer=mixer)
    return x
```

```python
import functools

import jax
import jax.numpy as jnp
from jax import lax
from jax.experimental import pallas as pl
from jax.experimental.pallas import tpu as pltpu

HEAD_DIM = 64
ROPE_DIM = 16
ROPE_THETA = 500000.0
EPS = 1e-6
MACARON_WEIGHT = 0.5
N_SUBLAYERS = 3
DIL_CONFIGS = ((128, 1), (512, 4), (2048, 16))
DIL_WINDOW_KEYS = 128

LANES = 128
MXU_DIM = 256
VMEM_LIMIT = 56 * 1024 * 1024

F32 = jnp.float32
BF16 = jnp.bfloat16


def _params(semantics, vmem=VMEM_LIMIT):
    return pltpu.CompilerParams(dimension_semantics=semantics, vmem_limit_bytes=vmem)


def _resident(block_shape, index_map):
    return pl.BlockSpec(block_shape, index_map, pipeline_mode=pl.Buffered(1))


def _dot(a, b):
    return jnp.dot(a, b, preferred_element_type=F32)


def _dot_nt(a, b):
    return lax.dot_general(a, b, (((1,), (1,)), ((), ())), preferred_element_type=F32)


def _sigmoid(x):
    return 1.0 / (1.0 + jnp.exp(-x))


def _norm_mod(x, g, shift, scale):
    ms = jnp.mean(x * x, axis=-1, keepdims=True)
    y = x * lax.rsqrt(ms + EPS) * g
    return y * (1.0 + scale) + shift


def _head_block_ones(n):
    r = lax.broadcasted_iota(jnp.int32, (n, n), 0) // HEAD_DIM
    c = lax.broadcasted_iota(jnp.int32, (n, n), 1) // HEAD_DIM
    return jnp.where(r == c, 1.0, 0.0).astype(BF16)


def _head_rms_norm(x, gain):
    ones = _head_block_ones(MXU_DIM)
    outs = []
    for c in range(x.shape[1] // MXU_DIM):
        xc = x[:, c * MXU_DIM:(c + 1) * MXU_DIM]
        ssq = _dot((xc * xc).astype(BF16), ones)
        r = lax.rsqrt(ssq * (1.0 / HEAD_DIM) + EPS)
        outs.append(xc * r * gain[:, c * MXU_DIM:(c + 1) * MXU_DIM])
    return jnp.concatenate(outs, axis=1)


def _mod_kernel(c_ref, w_ref, b_ref, o_ref):
    c = c_ref[...]
    c_act = c * _sigmoid(c)
    o_ref[0] = jnp.dot(c_act, w_ref[0], preferred_element_type=F32,
                       precision=lax.Precision.HIGHEST) + b_ref[0]


def _modulation(c, mod_w, mod_b):
    depth, d, n = mod_w.shape
    b = c.shape[0]
    tn = 1024
    return pl.pallas_call(
        _mod_kernel,
        grid=(depth, n // tn),
        in_specs=[
            pl.BlockSpec((b, d), lambda i, j: (0, 0)),
            pl.BlockSpec((1, d, tn), lambda i, j: (i, 0, j)),
            pl.BlockSpec((1, 1, tn), lambda i, j: (i, 0, j)),
        ],
        out_specs=pl.BlockSpec((1, b, tn), lambda i, j: (i, 0, j)),
        out_shape=jax.ShapeDtypeStruct((depth, b, n), F32),
        compiler_params=_params(("arbitrary", "arbitrary")),
        name="modulation",
    )(c, mod_w, mod_b.reshape(depth, 1, n))


FFN_CHUNK = 256


def _swiglu_residual(x, g_ref, sh_ref, sc_ref, gt_ref, wg_ref, wu_ref, wd_ref):
    h = _norm_mod(x, g_ref[...], sh_ref[0], sc_ref[0]).astype(BF16)
    acc = jnp.zeros_like(x)
    for c in range(wg_ref.shape[1] // FFN_CHUNK):
        cs = slice(c * FFN_CHUNK, (c + 1) * FFN_CHUNK)
        gg = _dot(h, wg_ref[:, cs])
        uu = _dot(h, wu_ref[:, cs])
        a = (gg * _sigmoid(gg) * uu).astype(BF16)
        acc = acc + _dot(a, wd_ref[cs, :])
    return x + (MACARON_WEIGHT * gt_ref[0]) * acc


def _ffn_kernel(x_ref, *refs):
    ffn_refs, o_ref = refs[:-1], refs[-1]
    o_ref[0] = _swiglu_residual(x_ref[0], *ffn_refs)


def _proj_ffn_kernel(x_ref, a_ref, wo_ref, gm_ref, *refs):
    ffn_refs, o_ref = refs[:-1], refs[-1]
    x = x_ref[0] + gm_ref[0] * _dot(a_ref[0], wo_ref[...])
    o_ref[0] = _swiglu_residual(x, *ffn_refs)


def _dil_ffn_kernel(x_ref, o0_ref, o1_ref, o2_ref, l0_ref, l1_ref, l2_ref, wo_ref, gm_ref, *refs):
    ffn_refs, o_ref, o_stage_ref, l_stage_ref = refs[:-3], refs[-3], refs[-2], refs[-1]
    merged = _dil_merge((o0_ref, o1_ref, o2_ref), (l0_ref, l1_ref, l2_ref),
                        o_stage_ref, l_stage_ref)
    x = x_ref[0] + gm_ref[0] * _dot(merged.astype(BF16), wo_ref[...])
    o_ref[0] = _swiglu_residual(x, *ffn_refs)


def _ffn(x, g, shift, scale, gate, wg, wu, wd, layer, which, mixer=None, tm=512):
    b, s, d = x.shape
    f = wg.shape[-1]
    row = pl.BlockSpec((1, 1, d), lambda i, j: (i, 0, 0))
    act = pl.BlockSpec((1, tm, d), lambda i, j: (i, j, 0))
    ffn_specs = [
        pl.BlockSpec((1, d), lambda i, j: (0, 0)),
        row, row, row,
        _resident((None, None, d, f), lambda i, j: (layer, which, 0, 0)),
        _resident((None, None, d, f), lambda i, j: (layer, which, 0, 0)),
        _resident((None, None, f, d), lambda i, j: (layer, which, 0, 0)),
    ]
    ffn_args = (g, shift, scale, gate, wg, wu, wd)
    scratch = []
    if mixer is None:
        body, mix_specs, mix_args = _ffn_kernel, [], ()
    elif len(mixer) == 3:
        o, w_out, gate_m = mixer
        body = _proj_ffn_kernel
        mix_specs = [act, _resident(w_out.shape, lambda i, j: (0, 0)), row]
        mix_args = (o, w_out.astype(BF16), gate_m)
    else:
        outs, lses, w_out, gate_m = mixer
        body = _dil_ffn_kernel
        grouped = lambda a: pl.BlockSpec((1, a.shape[1], tm // a.shape[1], a.shape[3]),
                                         lambda i, j: (i, 0, j, 0))
        mix_specs = ([grouped(a) for a in outs] + [grouped(a) for a in lses]
                     + [_resident(w_out.shape, lambda i, j: (0, 0)), row])
        mix_args = (*outs, *lses, w_out.astype(BF16), gate_m)
        scratch = [pltpu.VMEM((len(outs) * d // LANES, tm, LANES), F32),
                   pltpu.VMEM((len(lses), tm, LANES), F32)]
    return pl.pallas_call(
        body,
        grid=(b, s // tm),
        in_specs=[act] + mix_specs + ffn_specs,
        out_specs=act,
        out_shape=jax.ShapeDtypeStruct(x.shape, x.dtype),
        scratch_shapes=scratch,
        compiler_params=_params(("arbitrary", "arbitrary")),
        name="ffn",
    )(x, *mix_args, *ffn_args)


def _cumsum_rows(x):
    n = x.shape[0]
    row = lax.broadcasted_iota(jnp.int32, x.shape, 0)
    shift = 1
    while shift < n:
        x = x + jnp.where(row >= shift, pltpu.roll(x, shift, 0), 0.0)
        shift *= 2
    return x


N_SPLIT = 3
LOG2E = 1.4426950408889634


def _fox_in_kernel(x_ref, g_ref, sh_ref, sc_ref, wqk_ref, wvt_ref, wf_ref, bf_ref, qg_ref,
                   kg_ref, sel_ref, q_ref, ka_ref, vt_ref, carry_ref, *, n_heads):
    d = x_ref.shape[2]
    tm = x_ref.shape[1]

    @pl.when(pl.program_id(1) == 0)
    def _():
        carry_ref[...] = jnp.zeros_like(carry_ref)

    h = _norm_mod(x_ref[0], g_ref[...], sh_ref[0], sc_ref[0]).astype(BF16)
    z = _dot(h, wf_ref[...]) + bf_ref[...]
    logf = jnp.minimum(z, 0.0) - jnp.log1p(jnp.exp(-jnp.abs(z)))
    lane = lax.broadcasted_iota(jnp.int32, logf.shape, 1)
    logf = jnp.where(lane < n_heads, logf, 0.0)
    cum = _cumsum_rows(logf) + carry_ref[...]
    carry_ref[...] = cum[tm - 1:tm, :]

    q = _head_rms_norm(_dot(h, wqk_ref[:, 0:d]), qg_ref[...])
    q_ref[0] = (q * (LOG2E * HEAD_DIM ** -0.5)).astype(BF16)
    k = _head_rms_norm(_dot(h, wqk_ref[:, d:2 * d]), kg_ref[...]).astype(BF16)
    vt_ref[0] = _dot_nt(wvt_ref[...], h).astype(BF16)

    terms = []
    rest = cum * LOG2E
    for _ in range(N_SPLIT):
        t = rest.astype(BF16)
        terms.append(t)
        rest = rest - t.astype(F32)
    kx = _dot(jnp.concatenate(terms, axis=1), sel_ref[...]).astype(BF16)
    pieces = []
    for p in range(d // LANES):
        pieces += [k[:, p * LANES:(p + 1) * LANES], kx[:, p * LANES:(p + 1) * LANES]]
    ka_ref[0] = jnp.concatenate(pieces, axis=1)


def _fox_in(x, g, shift, scale, w_in, b_f, q_g, k_g, tm=512):
    b, s, d = x.shape
    n_heads = d // HEAD_DIM
    wqk = w_in[:, :2 * d].astype(BF16)
    wvt = w_in[:, 2 * d:3 * d].T.astype(BF16)
    wf = jnp.pad(w_in[:, 3 * d:], ((0, 0), (0, LANES - n_heads))).astype(BF16)
    bf = jnp.pad(b_f, (0, LANES - n_heads)).reshape(1, LANES)
    qg = jnp.tile(q_g, n_heads).reshape(1, d)
    kg = jnp.tile(k_g, n_heads).reshape(1, d)
    heads = jnp.arange(n_heads)
    sel = jnp.zeros((N_SPLIT * LANES, d), BF16)
    for t in range(N_SPLIT):
        sel = sel.at[t * LANES + heads, (heads // 2) * LANES + (heads % 2) * N_SPLIT + t].set(1.0)
    row = pl.BlockSpec((1, 1, d), lambda i, j: (i, 0, 0))
    full = lambda shp: pl.BlockSpec(shp, lambda i, j: (0, 0))
    res = lambda a: _resident(a.shape, lambda i, j: (0, 0))
    act = pl.BlockSpec((1, tm, d), lambda i, j: (i, j, 0))
    return pl.pallas_call(
        functools.partial(_fox_in_kernel, n_heads=n_heads),
        grid=(b, s // tm),
        in_specs=[act, full((1, d)), row, row, res(wqk), res(wvt), res(wf),
                  full((1, LANES)), full((1, d)), full((1, d)), res(sel)],
        out_specs=[act, pl.BlockSpec((1, tm, 2 * d), lambda i, j: (i, j, 0)),
                   pl.BlockSpec((1, d, tm), lambda i, j: (i, 0, j))],
        out_shape=[jax.ShapeDtypeStruct((b, s, d), BF16),
                   jax.ShapeDtypeStruct((b, s, 2 * d), BF16),
                   jax.ShapeDtypeStruct((b, d, s), BF16)],
        scratch_shapes=[pltpu.VMEM((1, LANES), F32)],
        compiler_params=_params(("arbitrary", "arbitrary")),
        name="fox_in",
    )(x, g, shift, scale, wqk, wvt, wf, bf, qg, kg, sel)


ONES_ROWS = 16
STEPS_PER_BLOCK = 4


def _fox_attn_kernel(q_ref, ka_ref, vt_ref, o_ref, qa_ref, m_ref, acc_ref, s_ref, *, tq, strip):
    tk = tq
    n_tiles = q_ref.shape[1] // tq
    lane = lax.broadcasted_iota(jnp.int32, (1, LANES), 1)
    q2 = q_ref[0]
    zero = jnp.zeros_like(q2)
    for hh in range(2):
        head = (lane < HEAD_DIM) if hh == 0 else (lane >= HEAD_DIM)
        pick = jnp.logical_and(lane >= hh * N_SPLIT, lane < (hh + 1) * N_SPLIT)
        minus_one = jnp.broadcast_to(jnp.where(pick, -1.0, 0.0), q2.shape).astype(BF16)
        qa_ref[hh] = jnp.concatenate([jnp.where(head, q2, zero), minus_one], axis=1)

    ones_rows = jnp.ones((ONES_ROWS, tk), BF16)
    chains = [(hh, slice(st * strip, (st + 1) * strip), st)
              for hh in range(2) for st in range(tq // strip)]
    n = len(chains)

    def n_keys(c, i, j):
        return (chains[c][2] + 1) * strip if i == j else tk

    def logits(c, i, j):
        hh, _, st = chains[c]
        keys = slice(j * tk, j * tk + n_keys(c, i, j))
        qrys = slice(i * tq + st * strip, i * tq + (st + 1) * strip)
        s = _dot_nt(ka_ref[0, keys, :], qa_ref[hh, qrys, :])
        if i == j:
            key = lax.broadcasted_iota(jnp.int32, s.shape, 0)
            qry = lax.broadcasted_iota(jnp.int32, s.shape, 1) + st * strip
            s = jnp.where(key <= qry, s, -jnp.inf)
        return s

    def softmax_pv(c, i, j):
        hh, cols, _ = chains[c]
        nk = n_keys(c, i, j)
        s = s_ref[c, :nk]
        m_prev = m_ref[hh, :, cols]
        m_new = jnp.maximum(m_prev, jnp.max(s, axis=0, keepdims=True))
        p = jnp.exp2(s - m_new)
        alpha = jnp.exp2(m_prev - m_new)
        m_ref[hh, :, cols] = m_new
        vt_ones = jnp.concatenate([vt_ref[0, :, j * tk:j * tk + nk], ones_rows[:, :nk]], axis=0)
        acc_ref[hh, :, cols] = alpha * acc_ref[hh, :, cols] + _dot(vt_ones, p.astype(BF16))

    def step(i, j, nxt):
        if j == 0:
            m_ref[...] = jnp.full_like(m_ref, -jnp.inf)
            acc_ref[...] = jnp.zeros_like(acc_ref)
        ahead = 2
        s_next = []
        for c in range(n + ahead):
            if nxt is not None and c < n:
                s_next.append(logits(c, *nxt))
            if c >= ahead:
                softmax_pv(c - ahead, i, j)
                if nxt is not None:
                    s_ref[c - ahead, :n_keys(c - ahead, *nxt)] = s_next[c - ahead]
        if j == i:
            dim = lax.broadcasted_iota(jnp.int32, (LANES, 1), 0)
            o_heads = [acc_ref[hh, :LANES, :] * (1.0 / acc_ref[hh, LANES:LANES + 1, :])
                       for hh in range(2)]
            o_ref[0, i * tq:(i + 1) * tq, :] = (
                jnp.where(dim < HEAD_DIM, o_heads[0], o_heads[1]).T.astype(o_ref.dtype))

    for c in range(n):
        s_ref[c, :n_keys(c, 0, 0)] = logits(c, 0, 0)

    sequence = [(i, j) for i in range(n_tiles) for j in range(i + 1)]
    always = pl.program_id(0) >= 0
    for start in range(0, len(sequence), STEPS_PER_BLOCK):

        @pl.when(always)
        def _(start=start):
            for idx in range(start, min(start + STEPS_PER_BLOCK, len(sequence))):
                nxt = sequence[idx + 1] if idx + 1 < len(sequence) else None
                step(*sequence[idx], nxt)


def _fox_attn(q, ka, vt, tq=512):
    b, s, d = q.shape
    qo = pl.BlockSpec((1, s, LANES), lambda bi, p: (bi, 0, p))
    return pl.pallas_call(
        functools.partial(_fox_attn_kernel, tq=tq, strip=MXU_DIM),
        grid=(b, d // LANES),
        in_specs=[qo, pl.BlockSpec((1, s, 2 * LANES), lambda bi, p: (bi, 0, p)),
                  pl.BlockSpec((1, LANES, s), lambda bi, p: (bi, p, 0))],
        out_specs=qo,
        out_shape=jax.ShapeDtypeStruct((b, s, d), BF16),
        scratch_shapes=[pltpu.VMEM((2, s, 2 * LANES), BF16),
                        pltpu.VMEM((2, 1, tq), F32),
                        pltpu.VMEM((2, LANES + ONES_ROWS, tq), F32),
                        pltpu.VMEM((2 * tq // MXU_DIM, tq, MXU_DIM), F32)],
        compiler_params=_params(("arbitrary", "arbitrary")),
        name="fox_attn",
    )(q, ka, vt)


def _rope_kernel(pos_ref, invf_ref, cos_ref, sin_ref):
    ang = pos_ref[0].astype(F32) * invf_ref[...]
    d = lax.broadcasted_iota(jnp.int32, ang.shape, 1) % HEAD_DIM
    half = ROPE_DIM // 2
    cos_ref[0] = jnp.where(d < ROPE_DIM, jnp.cos(ang), 1.0)
    sn = jnp.sin(ang)
    sin_ref[0] = jnp.where(d < half, -sn, jnp.where(d < ROPE_DIM, sn, 0.0))


def _rope_tables(positions, tm=512):
    b, s = positions.shape
    half = ROPE_DIM // 2
    inv_freq = ROPE_THETA ** (-(jnp.arange(half, dtype=F32) * 2.0 / ROPE_DIM))
    d = jnp.arange(LANES) % HEAD_DIM
    invf = jnp.where(d < ROPE_DIM, inv_freq[d % half], 0.0).reshape(1, LANES)
    tab = pl.BlockSpec((1, tm, LANES), lambda i, j: (i, j, 0))
    return pl.pallas_call(
        _rope_kernel,
        grid=(b, s // tm),
        in_specs=[pl.BlockSpec((1, tm, 1), lambda i, j: (i, j, 0)),
                  pl.BlockSpec((1, LANES), lambda i, j: (0, 0))],
        out_specs=[tab, tab],
        out_shape=[jax.ShapeDtypeStruct((b, s, LANES), F32)] * 2,
        compiler_params=_params(("arbitrary", "arbitrary")),
        name="rope_tables",
    )(positions.reshape(b, s, 1), invf)


def _rope(x, cos, sin):
    half = ROPE_DIM // 2
    d = lax.broadcasted_iota(jnp.int32, (1, LANES), 1) % HEAD_DIM
    outs = []
    for c in range(x.shape[1] // LANES):
        xc = x[:, c * LANES:(c + 1) * LANES]
        partner = jnp.where(d < half, pltpu.roll(xc, LANES - half, 1), pltpu.roll(xc, half, 1))
        outs.append(xc * cos + partner * sin)
    return jnp.concatenate(outs, axis=1)


def _dil_in_kernel(x_ref, g_ref, sh_ref, sc_ref, w_ref, qg_ref, kg_ref, cos_ref, sin_ref,
                   *refs):
    out_refs = refs
    tm, d = x_ref.shape[1:]
    h = _norm_mod(x_ref[0], g_ref[...], sh_ref[0], sc_ref[0]).astype(BF16)
    new = lax.broadcasted_iota(jnp.int32, (tm, tm), 0)
    old = lax.broadcasted_iota(jnp.int32, (tm, tm), 1)
    for grp, (_, dilation) in enumerate(DIL_CONFIGS):
        n = tm // dilation
        if dilation == 1:
            h_g, cos, sin = h, cos_ref[0], sin_ref[0]
        else:
            perm = jnp.where(old == (new % n) * dilation + new // n, 1.0, 0.0).astype(BF16)
            h_g = _dot(perm, h).astype(BF16)
            strided = lambda ref: jnp.concatenate(
                [ref[0, pl.ds(r, n, stride=dilation), :] for r in range(dilation)], axis=0)
            cos, sin = strided(cos_ref), strided(sin_ref)
        base = grp * 3 * d
        q = _head_rms_norm(_dot(h_g, w_ref[:, base:base + d]), qg_ref[grp:grp + 1, :])
        k = _head_rms_norm(_dot(h_g, w_ref[:, base + d:base + 2 * d]), kg_ref[grp:grp + 1, :])
        vals = (_rope(q, cos, sin) * (LOG2E * HEAD_DIM ** -0.5), _rope(k, cos, sin),
                _dot(h_g, w_ref[:, base + 2 * d:base + 3 * d]))
        for out_ref, val in zip(out_refs[3 * grp:3 * grp + 3], vals):
            val = val.astype(BF16)
            for r in range(dilation):
                out_ref[0, r] = val[r * n:(r + 1) * n]


def _dil_in(x, g, shift, scale, w_in, q_g, k_g, cos, sin, tm=256):
    b, s, d = x.shape
    n_groups = len(DIL_CONFIGS)
    n_heads = d // HEAD_DIM
    qg = jnp.tile(q_g, (1, n_heads))
    kg = jnp.tile(k_g, (1, n_heads))
    row = pl.BlockSpec((1, 1, d), lambda i, j: (i, 0, 0))
    full = lambda shp: pl.BlockSpec(shp, lambda i, j: (0, 0))
    act = pl.BlockSpec((1, tm, d), lambda i, j: (i, j, 0))
    tab = pl.BlockSpec((1, tm, LANES), lambda i, j: (i, j, 0))
    w = w_in.astype(BF16)
    out_specs, out_shape = [], []
    for _, dil in DIL_CONFIGS:
        out_specs += [pl.BlockSpec((1, dil, tm // dil, d), lambda i, j: (i, 0, j, 0))] * 3
        out_shape += [jax.ShapeDtypeStruct((b, dil, s // dil, d), BF16)] * 3
    return pl.pallas_call(
        _dil_in_kernel,
        grid=(b, s // tm),
        in_specs=[act, full((1, d)), row, row, _resident(w.shape, lambda i, j: (0, 0)),
                  full((n_groups, d)), full((n_groups, d)), tab, tab],
        out_specs=out_specs,
        out_shape=out_shape,
        compiler_params=_params(("arbitrary", "arbitrary")),
        name="dil_in",
    )(x, g, shift, scale, w, qg, kg, cos, sin)


def _dil_attn_kernel(q_ref, kc_ref, kp_ref, vc_ref, vp_ref, o_ref, lse_ref, lse_t_ref):
    w = DIL_WINDOW_KEYS
    n_res, tq, d = q_ref.shape[1:]
    i = pl.program_id(2)
    lane = lax.broadcasted_iota(jnp.int32, (1, LANES), 1)
    key = lax.broadcasted_iota(jnp.int32, (2 * w, 2 * w), 0)
    qry = lax.broadcasted_iota(jnp.int32, (2 * w, 2 * w), 1) % w
    band = jnp.logical_and(key >= qry, key <= qry + w)
    bias_band = jnp.where(band, 0.0, -jnp.inf)
    in_seq = jnp.logical_or(key >= w, i > 0)
    bias_first = jnp.where(jnp.logical_and(band, in_seq), 0.0, -jnp.inf)
    dim = lax.broadcasted_iota(jnp.int32, (LANES, 1), 0)

    chains = [(res, p, sub) for res in range(n_res) for p in range(d // LANES)
              for sub in range(tq // w)]
    head_a = jnp.where(lane < HEAD_DIM, 1.0, 0.0).astype(BF16)
    head_b = jnp.where(lane >= HEAD_DIM, 1.0, 0.0).astype(BF16)
    ones_rows = jnp.ones((ONES_ROWS, 2 * w), BF16)

    def window(cur_ref, prev_ref, res, p, sub):
        cs = slice(p * LANES, (p + 1) * LANES)
        if sub == 0:
            return jnp.concatenate([prev_ref[0, res, :, cs], cur_ref[0, res, :w, cs]], axis=0)
        return cur_ref[0, res, (sub - 1) * w:(sub + 1) * w, cs]

    def logits(c):
        res, p, sub = chains[c]
        qs = q_ref[0, res, sub * w:(sub + 1) * w, p * LANES:(p + 1) * LANES]
        q_both = jnp.concatenate([qs * head_a, qs * head_b], axis=0)
        s = _dot_nt(window(kc_ref, kp_ref, res, p, sub), q_both)
        return s + (bias_first if sub == 0 else bias_band)

    def finish(c, s):
        res, p, sub = chains[c]
        rows = slice(sub * w, (sub + 1) * w)
        m = jnp.max(s, axis=0, keepdims=True)
        e = jnp.exp2(s - m)
        v_t = jnp.concatenate([window(vc_ref, vp_ref, res, p, sub).T, ones_rows], axis=0)
        o_l = _dot(v_t, e.astype(BF16))
        l = o_l[LANES:LANES + 1]
        o_t = o_l[:LANES] * (1.0 / l)
        o_ref[0, res, rows, p * LANES:(p + 1) * LANES] = (
            jnp.where(dim < HEAD_DIM, o_t[:, :w], o_t[:, w:]).T)
        lse = m + jnp.log(l) * LOG2E
        lse_t_ref[res, 2 * p:2 * p + 1, rows] = lse[:, :w]
        lse_t_ref[res, 2 * p + 1:2 * p + 2, rows] = lse[:, w:]

    lse_t_ref[...] = jnp.zeros_like(lse_t_ref)
    ahead = 4
    pending = [logits(c) for c in range(ahead)]
    for c in range(len(chains)):
        if c + ahead < len(chains):
            pending.append(logits(c + ahead))
        finish(c, pending[c])
    for res in range(n_res):
        lse_ref[0, res] = lse_t_ref[res].T


def _dil_attn(q, k, v, rows_per_step=512):
    b, dilation, sub_len, d = q.shape
    w = DIL_WINDOW_KEYS
    tq = min(rows_per_step, sub_len)
    n_res = min(rows_per_step // tq, dilation)
    cur = pl.BlockSpec((1, n_res, tq, d), lambda bi, r, i: (bi, r, i, 0))
    prev = pl.BlockSpec((1, n_res, w, d),
                        lambda bi, r, i: (bi, r, jnp.maximum(i * (tq // w) - 1, 0), 0))
    return pl.pallas_call(
        _dil_attn_kernel,
        grid=(b, dilation // n_res, sub_len // tq),
        in_specs=[cur, cur, prev, cur, prev],
        out_specs=[cur, pl.BlockSpec((1, n_res, tq, LANES), lambda bi, r, i: (bi, r, i, 0))],
        out_shape=[jax.ShapeDtypeStruct(q.shape, F32),
                   jax.ShapeDtypeStruct((b, dilation, sub_len, LANES), F32)],
        scratch_shapes=[pltpu.VMEM((n_res, LANES, tq), F32)],
        compiler_params=_params(("arbitrary", "arbitrary", "arbitrary")),
        name="dil_attn",
    )(q, k, k, v, v)


def _dil_merge(o_refs, l_refs, o_stage_ref, l_stage_ref):
    d = o_refs[0].shape[3]

    def token_order(ref, stage_ref, slot):
        dilation, n, width = ref.shape[1:]
        if dilation == 1:
            return ref[0, 0]
        n_tiles = width // LANES
        for r in range(dilation):
            for t in range(n_tiles):
                stage_ref[slot * n_tiles + t, pl.ds(r, n, stride=dilation), :] = (
                    ref[0, r, :, t * LANES:(t + 1) * LANES])
        return jnp.concatenate([stage_ref[slot * n_tiles + t] for t in range(n_tiles)], axis=1)

    lses = [token_order(ref, l_stage_ref, g) for g, ref in enumerate(l_refs)]
    outs = [token_order(ref, o_stage_ref, g) for g, ref in enumerate(o_refs)]
    m = jnp.maximum(jnp.maximum(lses[0], lses[1]), lses[2])
    es = [jnp.exp2(l - m) for l in lses]
    inv = 1.0 / (es[0] + es[1] + es[2])
    r = lax.broadcasted_iota(jnp.int32, (LANES, d), 0)
    c = lax.broadcasted_iota(jnp.int32, (LANES, d), 1) // HEAD_DIM
    expand = jnp.where(r == c, 1.0, 0.0).astype(BF16)
    merged = outs[-1]
    for e, o in zip(es[:-1], outs[:-1]):
        alpha = e * inv
        hi = alpha.astype(BF16)
        lo = (alpha - hi.astype(F32)).astype(BF16)
        merged = merged + (_dot(hi, expand) + _dot(lo, expand)) * (o - outs[-1])
    return merged


def kernel(x, c, positions, mod_w, mod_b, norm_g, ffn_w_gate, ffn_w_up, ffn_w_down,
           fox_w_in, fox_b_f, fox_q_g, fox_k_g, fox_w_out,
           dil_w_in, dil_q_g, dil_k_g, dil_w_out):
    b, s, d = x.shape
    depth = mod_w.shape[0]
    mod = _modulation(c, mod_w, mod_b).reshape(depth, b, N_SUBLAYERS, 3, 1, d)
    wg = ffn_w_gate.astype(BF16)
    wu = ffn_w_up.astype(BF16)
    wd = ffn_w_down.astype(BF16)
    cos, sin = _rope_tables(positions)

    for i in range(depth):
        shift = lambda sub: mod[i, :, sub, 0]
        scale = lambda sub: mod[i, :, sub, 1]
        gate = lambda sub: mod[i, :, sub, 2]
        g = lambda sub: norm_g[i, sub].reshape(1, d)

        x = _ffn(x, g(0), shift(0), scale(0), gate(0), wg, wu, wd, i, 0)
        j = i // 2
        if i % 2 == 0:
            q, ka, vt = _fox_in(x, g(1), shift(1), scale(1), fox_w_in[j], fox_b_f[j],
                                fox_q_g[j], fox_k_g[j])
            mixer = (_fox_attn(q, ka, vt), fox_w_out[j], gate(1))
        else:
            qkv = _dil_in(x, g(1), shift(1), scale(1), dil_w_in[j], dil_q_g[j], dil_k_g[j],
                          cos, sin)
            outs, lses = [], []
            for grp in range(len(DIL_CONFIGS)):
                o, lse = _dil_attn(qkv[3 * grp], qkv[3 * grp + 1], qkv[3 * grp + 2])
                outs.append(o)
                lses.append(lse)
            mixer = (outs, lses, dil_w_out[j], gate(1))
        x = _ffn(x, g(2), shift(2), scale(2), gate(2), wg, wu, wd, i, 1, mixer=mixer)
    return x
```

```python
import functools

import jax
import jax.numpy as jnp
from jax import lax
from jax.experimental import pallas as pl
from jax.experimental.pallas import tpu as pltpu

HEAD_DIM = 64
ROPE_DIM = 16
ROPE_THETA = 500000.0
EPS = 1e-6
MACARON_WEIGHT = 0.5
N_SUBLAYERS = 3
DIL_CONFIGS = ((128, 1), (512, 4), (2048, 16))
DIL_WINDOW_KEYS = 128

LANES = 128
MXU_DIM = 256
VMEM_LIMIT = 56 * 1024 * 1024

F32 = jnp.float32
BF16 = jnp.bfloat16


def _params(semantics, vmem=VMEM_LIMIT):
    return pltpu.CompilerParams(dimension_semantics=semantics, vmem_limit_bytes=vmem)


def _resident(block_shape, index_map):
    return pl.BlockSpec(block_shape, index_map, pipeline_mode=pl.Buffered(1))


def _dot(a, b):
    return jnp.dot(a, b, preferred_element_type=F32)


def _dot_nt(a, b):
    return lax.dot_general(a, b, (((1,), (1,)), ((), ())), preferred_element_type=F32)


def _sigmoid(x):
    return 1.0 / (1.0 + jnp.exp(-x))


def _norm_mod(x, g, shift, scale):
    ms = jnp.mean(x * x, axis=-1, keepdims=True)
    y = x * lax.rsqrt(ms + EPS) * g
    return y * (1.0 + scale) + shift


def _head_block_ones(n):
    r = lax.broadcasted_iota(jnp.int32, (n, n), 0) // HEAD_DIM
    c = lax.broadcasted_iota(jnp.int32, (n, n), 1) // HEAD_DIM
    return jnp.where(r == c, 1.0, 0.0).astype(BF16)


def _head_rms_norm(x, gain):
    ones = _head_block_ones(MXU_DIM)
    outs = []
    for c in range(x.shape[1] // MXU_DIM):
        xc = x[:, c * MXU_DIM:(c + 1) * MXU_DIM]
        ssq = _dot((xc * xc).astype(BF16), ones)
        r = lax.rsqrt(ssq * (1.0 / HEAD_DIM) + EPS)
        outs.append(xc * r * gain[:, c * MXU_DIM:(c + 1) * MXU_DIM])
    return jnp.concatenate(outs, axis=1)


def _mod_kernel(c_ref, w_ref, b_ref, o_ref):
    c = c_ref[...]
    c_act = c * _sigmoid(c)
    o_ref[0] = jnp.dot(c_act, w_ref[0], preferred_element_type=F32,
                       precision=lax.Precision.HIGHEST) + b_ref[0]


def _modulation(c, mod_w, mod_b):
    depth, d, n = mod_w.shape
    b = c.shape[0]
    tn = 1024
    return pl.pallas_call(
        _mod_kernel,
        grid=(depth, n // tn),
        in_specs=[
            pl.BlockSpec((b, d), lambda i, j: (0, 0)),
            pl.BlockSpec((1, d, tn), lambda i, j: (i, 0, j)),
            pl.BlockSpec((1, 1, tn), lambda i, j: (i, 0, j)),
        ],
        out_specs=pl.BlockSpec((1, b, tn), lambda i, j: (i, 0, j)),
        out_shape=jax.ShapeDtypeStruct((depth, b, n), F32),
        compiler_params=_params(("arbitrary", "arbitrary")),
        name="modulation",
    )(c, mod_w, mod_b.reshape(depth, 1, n))


FFN_CHUNK = 256


def _swiglu_residual(x, g_ref, sh_ref, sc_ref, gt_ref, wg_ref, wu_ref, wd_ref):
    h = _norm_mod(x, g_ref[...], sh_ref[0], sc_ref[0]).astype(BF16)
    acc = jnp.zeros_like(x)
    for c in range(wg_ref.shape[1] // FFN_CHUNK):
        cs = slice(c * FFN_CHUNK, (c + 1) * FFN_CHUNK)
        gg = _dot(h, wg_ref[:, cs])
        uu = _dot(h, wu_ref[:, cs])
        a = (gg * _sigmoid(gg) * uu).astype(BF16)
        acc = acc + _dot(a, wd_ref[cs, :])
    return x + (MACARON_WEIGHT * gt_ref[0]) * acc


def _ffn_kernel(x_ref, *refs):
    ffn_refs, o_ref = refs[:-1], refs[-1]
    o_ref[0] = _swiglu_residual(x_ref[0], *ffn_refs)


def _proj_ffn_kernel(x_ref, a_ref, wo_ref, gm_ref, *refs):
    ffn_refs, o_ref = refs[:-1], refs[-1]
    x = x_ref[0] + gm_ref[0] * _dot(a_ref[0], wo_ref[...])
    o_ref[0] = _swiglu_residual(x, *ffn_refs)


def _dil_ffn_kernel(x_ref, o0_ref, o1_ref, o2_ref, l0_ref, l1_ref, l2_ref, wo_ref, gm_ref, *refs):
    ffn_refs, o_ref, o_stage_ref, l_stage_ref = refs[:-3], refs[-3], refs[-2], refs[-1]
    merged = _dil_merge((o0_ref, o1_ref, o2_ref), (l0_ref, l1_ref, l2_ref),
                        o_stage_ref, l_stage_ref)
    x = x_ref[0] + gm_ref[0] * _dot(merged.astype(BF16), wo_ref[...])
    o_ref[0] = _swiglu_residual(x, *ffn_refs)


def _ffn(x, g, shift, scale, gate, wg, wu, wd, layer, which, mixer=None, tm=512):
    b, s, d = x.shape
    f = wg.shape[-1]
    row = pl.BlockSpec((1, 1, d), lambda i, j: (i, 0, 0))
    act = pl.BlockSpec((1, tm, d), lambda i, j: (i, j, 0))
    ffn_specs = [
        pl.BlockSpec((1, d), lambda i, j: (0, 0)),
        row, row, row,
        _resident((None, None, d, f), lambda i, j: (layer, which, 0, 0)),
        _resident((None, None, d, f), lambda i, j: (layer, which, 0, 0)),
        _resident((None, None, f, d), lambda i, j: (layer, which, 0, 0)),
    ]
    ffn_args = (g, shift, scale, gate, wg, wu, wd)
    scratch = []
    if mixer is None:
        body, mix_specs, mix_args = _ffn_kernel, [], ()
    elif len(mixer) == 3:
        o, w_out, gate_m = mixer
        body = _proj_ffn_kernel
        mix_specs = [act, _resident(w_out.shape, lambda i, j: (0, 0)), row]
        mix_args = (o, w_out.astype(BF16), gate_m)
    else:
        outs, lses, w_out, gate_m = mixer
        body = _dil_ffn_kernel
        grouped = lambda a: pl.BlockSpec((1, a.shape[1], tm // a.shape[1], a.shape[3]),
                                         lambda i, j: (i, 0, j, 0))
        mix_specs = ([grouped(a) for a in outs] + [grouped(a) for a in lses]
                     + [_resident(w_out.shape, lambda i, j: (0, 0)), row])
        mix_args = (*outs, *lses, w_out.astype(BF16), gate_m)
        scratch = [pltpu.VMEM((len(outs) * d // LANES, tm, LANES), F32),
                   pltpu.VMEM((len(lses), tm, LANES), F32)]
    return pl.pallas_call(
        body,
        grid=(b, s // tm),
        in_specs=[act] + mix_specs + ffn_specs,
        out_specs=act,
        out_shape=jax.ShapeDtypeStruct(x.shape, x.dtype),
        scratch_shapes=scratch,
        compiler_params=_params(("arbitrary", "arbitrary")),
        name="ffn",
    )(x, *mix_args, *ffn_args)


def _cumsum_rows(x):
    n = x.shape[0]
    row = lax.broadcasted_iota(jnp.int32, x.shape, 0)
    shift = 1
    while shift < n:
        x = x + jnp.where(row >= shift, pltpu.roll(x, shift, 0), 0.0)
        shift *= 2
    return x


N_SPLIT = 3
LOG2E = 1.4426950408889634


def _fox_in_kernel(x_ref, g_ref, sh_ref, sc_ref, wqk_ref, wvt_ref, wf_ref, bf_ref, qg_ref,
                   kg_ref, sel_ref, q_ref, ka_ref, vt_ref, carry_ref, *, n_heads):
    d = x_ref.shape[2]
    tm = x_ref.shape[1]

    @pl.when(pl.program_id(1) == 0)
    def _():
        carry_ref[...] = jnp.zeros_like(carry_ref)

    h = _norm_mod(x_ref[0], g_ref[...], sh_ref[0], sc_ref[0]).astype(BF16)
    z = _dot(h, wf_ref[...]) + bf_ref[...]
    logf = jnp.minimum(z, 0.0) - jnp.log1p(jnp.exp(-jnp.abs(z)))
    lane = lax.broadcasted_iota(jnp.int32, logf.shape, 1)
    logf = jnp.where(lane < n_heads, logf, 0.0)
    cum = _cumsum_rows(logf) + carry_ref[...]
    carry_ref[...] = cum[tm - 1:tm, :]

    q = _head_rms_norm(_dot(h, wqk_ref[:, 0:d]), qg_ref[...])
    q_ref[0] = (q * (LOG2E * HEAD_DIM ** -0.5)).astype(BF16)
    k = _head_rms_norm(_dot(h, wqk_ref[:, d:2 * d]), kg_ref[...]).astype(BF16)
    vt_ref[0] = _dot_nt(wvt_ref[...], h).astype(BF16)

    terms = []
    rest = cum * LOG2E
    for _ in range(N_SPLIT):
        t = rest.astype(BF16)
        terms.append(t)
        rest = rest - t.astype(F32)
    kx = _dot(jnp.concatenate(terms, axis=1), sel_ref[...]).astype(BF16)
    pieces = []
    for p in range(d // LANES):
        pieces += [k[:, p * LANES:(p + 1) * LANES], kx[:, p * LANES:(p + 1) * LANES]]
    ka_ref[0] = jnp.concatenate(pieces, axis=1)


def _fox_in(x, g, shift, scale, w_in, b_f, q_g, k_g, tm=512):
    b, s, d = x.shape
    n_heads = d // HEAD_DIM
    wqk = w_in[:, :2 * d].astype(BF16)
    wvt = w_in[:, 2 * d:3 * d].T.astype(BF16)
    wf = jnp.pad(w_in[:, 3 * d:], ((0, 0), (0, LANES - n_heads))).astype(BF16)
    bf = jnp.pad(b_f, (0, LANES - n_heads)).reshape(1, LANES)
    qg = jnp.tile(q_g, n_heads).reshape(1, d)
    kg = jnp.tile(k_g, n_heads).reshape(1, d)
    heads = jnp.arange(n_heads)
    sel = jnp.zeros((N_SPLIT * LANES, d), BF16)
    for t in range(N_SPLIT):
        sel = sel.at[t * LANES + heads, (heads // 2) * LANES + (heads % 2) * N_SPLIT + t].set(1.0)
    row = pl.BlockSpec((1, 1, d), lambda i, j: (i, 0, 0))
    full = lambda shp: pl.BlockSpec(shp, lambda i, j: (0, 0))
    res = lambda a: _resident(a.shape, lambda i, j: (0, 0))
    act = pl.BlockSpec((1, tm, d), lambda i, j: (i, j, 0))
    return pl.pallas_call(
        functools.partial(_fox_in_kernel, n_heads=n_heads),
        grid=(b, s // tm),
        in_specs=[act, full((1, d)), row, row, res(wqk), res(wvt), res(wf),
                  full((1, LANES)), full((1, d)), full((1, d)), res(sel)],
        out_specs=[act, pl.BlockSpec((1, tm, 2 * d), lambda i, j: (i, j, 0)),
                   pl.BlockSpec((1, d, tm), lambda i, j: (i, 0, j))],
        out_shape=[jax.ShapeDtypeStruct((b, s, d), BF16),
                   jax.ShapeDtypeStruct((b, s, 2 * d), BF16),
                   jax.ShapeDtypeStruct((b, d, s), BF16)],
        scratch_shapes=[pltpu.VMEM((1, LANES), F32)],
        compiler_params=_params(("arbitrary", "arbitrary")),
        name="fox_in",
    )(x, g, shift, scale, wqk, wvt, wf, bf, qg, kg, sel)


ONES_ROWS = 16


def _fox_attn_kernel(q_ref, ka_ref, vt_ref, o_ref, qa_ref, m_ref, acc_ref, s_ref, *, tq, strip):
    tk = tq
    n_tiles = q_ref.shape[1] // tq
    lane = lax.broadcasted_iota(jnp.int32, (1, LANES), 1)
    q2 = q_ref[0]
    zero = jnp.zeros_like(q2)
    for hh in range(2):
        head = (lane < HEAD_DIM) if hh == 0 else (lane >= HEAD_DIM)
        pick = jnp.logical_and(lane >= hh * N_SPLIT, lane < (hh + 1) * N_SPLIT)
        minus_one = jnp.broadcast_to(jnp.where(pick, -1.0, 0.0), q2.shape).astype(BF16)
        qa_ref[hh] = jnp.concatenate([jnp.where(head, q2, zero), minus_one], axis=1)

    ones_rows = jnp.ones((ONES_ROWS, tk), BF16)
    chains = [(hh, slice(st * strip, (st + 1) * strip), st)
              for hh in range(2) for st in range(tq // strip)]
    n = len(chains)

    def n_keys(c, i, j):
        return (chains[c][2] + 1) * strip if i == j else tk

    def logits(c, i, j):
        hh, _, st = chains[c]
        keys = slice(j * tk, j * tk + n_keys(c, i, j))
        qrys = slice(i * tq + st * strip, i * tq + (st + 1) * strip)
        s = _dot_nt(ka_ref[0, keys, :], qa_ref[hh, qrys, :])
        if i == j:
            key = lax.broadcasted_iota(jnp.int32, s.shape, 0)
            qry = lax.broadcasted_iota(jnp.int32, s.shape, 1) + st * strip
            s = jnp.where(key <= qry, s, -jnp.inf)
        return s

    def softmax_pv(c, i, j):
        hh, cols, _ = chains[c]
        nk = n_keys(c, i, j)
        s = s_ref[c, :nk]
        m_prev = m_ref[hh, :, cols]
        m_new = jnp.maximum(m_prev, jnp.max(s, axis=0, keepdims=True))
        p = jnp.exp2(s - m_new)
        alpha = jnp.exp2(m_prev - m_new)
        m_ref[hh, :, cols] = m_new
        v_t = vt_ref[0, hh * HEAD_DIM:(hh + 1) * HEAD_DIM, j * tk:j * tk + nk]
        vt_ones = jnp.concatenate([v_t, ones_rows[:, :nk]], axis=0)
        acc_ref[hh, :, cols] = alpha * acc_ref[hh, :, cols] + _dot(vt_ones, p.astype(BF16))

    def step(i, j, nxt):
        if j == 0:
            m_ref[...] = jnp.full_like(m_ref, -jnp.inf)
            acc_ref[...] = jnp.zeros_like(acc_ref)
        ahead = 2
        s_next = []
        for c in range(n + ahead):
            if nxt is not None and c < n:
                s_next.append(logits(c, *nxt))
            if c >= ahead:
                softmax_pv(c - ahead, i, j)
                if nxt is not None:
                    s_ref[c - ahead, :n_keys(c - ahead, *nxt)] = s_next[c - ahead]
        if j == i:
            o_heads = [acc_ref[hh, :HEAD_DIM, :] * (1.0 / acc_ref[hh, HEAD_DIM:HEAD_DIM + 1, :])
                       for hh in range(2)]
            o_ref[0, i * tq:(i + 1) * tq, :] = (
                jnp.concatenate(o_heads, axis=0).T.astype(o_ref.dtype))

    for c in range(n):
        s_ref[c, :n_keys(c, 0, 0)] = logits(c, 0, 0)

    sequence = [(i, j) for i in range(n_tiles) for j in range(i + 1)]
    for idx, (i, j) in enumerate(sequence):
        step(i, j, sequence[idx + 1] if idx + 1 < len(sequence) else None)


def _fox_attn(q, ka, vt, tq=512):
    b, s, d = q.shape
    qo = pl.BlockSpec((1, s, LANES), lambda bi, p: (bi, 0, p))
    return pl.pallas_call(
        functools.partial(_fox_attn_kernel, tq=tq, strip=MXU_DIM),
        grid=(b, d // LANES),
        in_specs=[qo, pl.BlockSpec((1, s, 2 * LANES), lambda bi, p: (bi, 0, p)),
                  pl.BlockSpec((1, LANES, s), lambda bi, p: (bi, p, 0))],
        out_specs=qo,
        out_shape=jax.ShapeDtypeStruct((b, s, d), BF16),
        scratch_shapes=[pltpu.VMEM((2, s, 2 * LANES), BF16),
                        pltpu.VMEM((2, 1, tq), F32),
                        pltpu.VMEM((2, HEAD_DIM + ONES_ROWS, tq), F32),
                        pltpu.VMEM((2 * tq // MXU_DIM, tq, MXU_DIM), F32)],
        compiler_params=_params(("arbitrary", "arbitrary")),
        name="fox_attn",
    )(q, ka, vt)


def _rope_kernel(pos_ref, invf_ref, cos_ref, sin_ref):
    ang = pos_ref[0].astype(F32) * invf_ref[...]
    d = lax.broadcasted_iota(jnp.int32, ang.shape, 1) % HEAD_DIM
    half = ROPE_DIM // 2
    cos_ref[0] = jnp.where(d < ROPE_DIM, jnp.cos(ang), 1.0)
    sn = jnp.sin(ang)
    sin_ref[0] = jnp.where(d < half, -sn, jnp.where(d < ROPE_DIM, sn, 0.0))


def _rope_tables(positions, tm=512):
    b, s = positions.shape
    half = ROPE_DIM // 2
    inv_freq = ROPE_THETA ** (-(jnp.arange(half, dtype=F32) * 2.0 / ROPE_DIM))
    d = jnp.arange(LANES) % HEAD_DIM
    invf = jnp.where(d < ROPE_DIM, inv_freq[d % half], 0.0).reshape(1, LANES)
    tab = pl.BlockSpec((1, tm, LANES), lambda i, j: (i, j, 0))
    return pl.pallas_call(
        _rope_kernel,
        grid=(b, s // tm),
        in_specs=[pl.BlockSpec((1, tm, 1), lambda i, j: (i, j, 0)),
                  pl.BlockSpec((1, LANES), lambda i, j: (0, 0))],
        out_specs=[tab, tab],
        out_shape=[jax.ShapeDtypeStruct((b, s, LANES), F32)] * 2,
        compiler_params=_params(("arbitrary", "arbitrary")),
        name="rope_tables",
    )(positions.reshape(b, s, 1), invf)


def _rope(x, cos, sin):
    half = ROPE_DIM // 2
    d = lax.broadcasted_iota(jnp.int32, (1, LANES), 1) % HEAD_DIM
    outs = []
    for c in range(x.shape[1] // LANES):
        xc = x[:, c * LANES:(c + 1) * LANES]
        partner = jnp.where(d < half, pltpu.roll(xc, LANES - half, 1), pltpu.roll(xc, half, 1))
        outs.append(xc * cos + partner * sin)
    return jnp.concatenate(outs, axis=1)


def _dil_in_kernel(x_ref, g_ref, sh_ref, sc_ref, w_ref, qg_ref, kg_ref, cos_ref, sin_ref,
                   *refs):
    out_refs = refs
    tm, d = x_ref.shape[1:]
    h = _norm_mod(x_ref[0], g_ref[...], sh_ref[0], sc_ref[0]).astype(BF16)
    new = lax.broadcasted_iota(jnp.int32, (tm, tm), 0)
    old = lax.broadcasted_iota(jnp.int32, (tm, tm), 1)
    for grp, (_, dilation) in enumerate(DIL_CONFIGS):
        n = tm // dilation
        if dilation == 1:
            h_g, cos, sin = h, cos_ref[0], sin_ref[0]
        else:
            perm = jnp.where(old == (new % n) * dilation + new // n, 1.0, 0.0).astype(BF16)
            h_g = _dot(perm, h).astype(BF16)
            strided = lambda ref: jnp.concatenate(
                [ref[0, pl.ds(r, n, stride=dilation), :] for r in range(dilation)], axis=0)
            cos, sin = strided(cos_ref), strided(sin_ref)
        base = grp * 3 * d
        q = _head_rms_norm(_dot(h_g, w_ref[:, base:base + d]), qg_ref[grp:grp + 1, :])
        k = _head_rms_norm(_dot(h_g, w_ref[:, base + d:base + 2 * d]), kg_ref[grp:grp + 1, :])
        vals = (_rope(q, cos, sin) * (LOG2E * HEAD_DIM ** -0.5), _rope(k, cos, sin),
                _dot(h_g, w_ref[:, base + 2 * d:base + 3 * d]))
        for out_ref, val in zip(out_refs[3 * grp:3 * grp + 3], vals):
            val = val.astype(BF16)
            for r in range(dilation):
                out_ref[0, r] = val[r * n:(r + 1) * n]


def _dil_in(x, g, shift, scale, w_in, q_g, k_g, cos, sin, tm=256):
    b, s, d = x.shape
    n_groups = len(DIL_CONFIGS)
    n_heads = d // HEAD_DIM
    qg = jnp.tile(q_g, (1, n_heads))
    kg = jnp.tile(k_g, (1, n_heads))
    row = pl.BlockSpec((1, 1, d), lambda i, j: (i, 0, 0))
    full = lambda shp: pl.BlockSpec(shp, lambda i, j: (0, 0))
    act = pl.BlockSpec((1, tm, d), lambda i, j: (i, j, 0))
    tab = pl.BlockSpec((1, tm, LANES), lambda i, j: (i, j, 0))
    w = w_in.astype(BF16)
    out_specs, out_shape = [], []
    for _, dil in DIL_CONFIGS:
        out_specs += [pl.BlockSpec((1, dil, tm // dil, d), lambda i, j: (i, 0, j, 0))] * 3
        out_shape += [jax.ShapeDtypeStruct((b, dil, s // dil, d), BF16)] * 3
    return pl.pallas_call(
        _dil_in_kernel,
        grid=(b, s // tm),
        in_specs=[act, full((1, d)), row, row, _resident(w.shape, lambda i, j: (0, 0)),
                  full((n_groups, d)), full((n_groups, d)), tab, tab],
        out_specs=out_specs,
        out_shape=out_shape,
        compiler_params=_params(("arbitrary", "arbitrary")),
        name="dil_in",
    )(x, g, shift, scale, w, qg, kg, cos, sin)


def _dil_attn_kernel(q_ref, kc_ref, kp_ref, vc_ref, vp_ref, o_ref, lse_ref, lse_t_ref):
    w = DIL_WINDOW_KEYS
    n_res, tq, d = q_ref.shape[1:]
    i = pl.program_id(2)
    lane = lax.broadcasted_iota(jnp.int32, (1, LANES), 1)
    key = lax.broadcasted_iota(jnp.int32, (2 * w, 2 * w), 0)
    qry = lax.broadcasted_iota(jnp.int32, (2 * w, 2 * w), 1) % w
    band = jnp.logical_and(key >= qry, key <= qry + w)
    bias_band = jnp.where(band, 0.0, -jnp.inf)
    in_seq = jnp.logical_or(key >= w, i > 0)
    bias_first = jnp.where(jnp.logical_and(band, in_seq), 0.0, -jnp.inf)
    dim = lax.broadcasted_iota(jnp.int32, (LANES, 1), 0)

    chains = [(res, p, sub) for res in range(n_res) for p in range(d // LANES)
              for sub in range(tq // w)]
    head_a = jnp.where(lane < HEAD_DIM, 1.0, 0.0).astype(BF16)
    head_b = jnp.where(lane >= HEAD_DIM, 1.0, 0.0).astype(BF16)
    ones_rows = jnp.ones((ONES_ROWS, 2 * w), BF16)

    def window(cur_ref, prev_ref, res, p, sub):
        cs = slice(p * LANES, (p + 1) * LANES)
        if sub == 0:
            return jnp.concatenate([prev_ref[0, res, :, cs], cur_ref[0, res, :w, cs]], axis=0)
        return cur_ref[0, res, (sub - 1) * w:(sub + 1) * w, cs]

    def logits(c):
        res, p, sub = chains[c]
        qs = q_ref[0, res, sub * w:(sub + 1) * w, p * LANES:(p + 1) * LANES]
        q_both = jnp.concatenate([qs * head_a, qs * head_b], axis=0)
        s = _dot_nt(window(kc_ref, kp_ref, res, p, sub), q_both)
        return s + (bias_first if sub == 0 else bias_band)

    def finish(c, s):
        res, p, sub = chains[c]
        rows = slice(sub * w, (sub + 1) * w)
        m = jnp.max(s, axis=0, keepdims=True)
        e = jnp.exp2(s - m)
        v_t = jnp.concatenate([window(vc_ref, vp_ref, res, p, sub).T, ones_rows], axis=0)
        o_l = _dot(v_t, e.astype(BF16))
        l = o_l[LANES:LANES + 1]
        o_t = o_l[:LANES] * (1.0 / l)
        o_ref[0, res, rows, p * LANES:(p + 1) * LANES] = (
            jnp.where(dim < HEAD_DIM, o_t[:, :w], o_t[:, w:]).T)
        lse = m + jnp.log(l) * LOG2E
        lse_t_ref[res, 2 * p:2 * p + 1, rows] = lse[:, :w]
        lse_t_ref[res, 2 * p + 1:2 * p + 2, rows] = lse[:, w:]

    lse_t_ref[...] = jnp.zeros_like(lse_t_ref)
    ahead = 4
    pending = [logits(c) for c in range(ahead)]
    for c in range(len(chains)):
        if c + ahead < len(chains):
            pending.append(logits(c + ahead))
        finish(c, pending[c])
    for res in range(n_res):
        lse_ref[0, res] = lse_t_ref[res].T


def _dil_attn(q, k, v, rows_per_step=512):
    b, dilation, sub_len, d = q.shape
    w = DIL_WINDOW_KEYS
    tq = min(rows_per_step, sub_len)
    n_res = min(rows_per_step // tq, dilation)
    cur = pl.BlockSpec((1, n_res, tq, d), lambda bi, r, i: (bi, r, i, 0))
    prev = pl.BlockSpec((1, n_res, w, d),
                        lambda bi, r, i: (bi, r, jnp.maximum(i * (tq // w) - 1, 0), 0))
    return pl.pallas_call(
        _dil_attn_kernel,
        grid=(b, dilation // n_res, sub_len // tq),
        in_specs=[cur, cur, prev, cur, prev],
        out_specs=[cur, pl.BlockSpec((1, n_res, tq, LANES), lambda bi, r, i: (bi, r, i, 0))],
        out_shape=[jax.ShapeDtypeStruct(q.shape, F32),
                   jax.ShapeDtypeStruct((b, dilation, sub_len, LANES), F32)],
        scratch_shapes=[pltpu.VMEM((n_res, LANES, tq), F32)],
        compiler_params=_params(("arbitrary", "arbitrary", "arbitrary")),
        name="dil_attn",
    )(q, k, k, v, v)


def _dil_merge(o_refs, l_refs, o_stage_ref, l_stage_ref):
    d = o_refs[0].shape[3]

    def token_order(ref, stage_ref, slot):
        dilation, n, width = ref.shape[1:]
        if dilation == 1:
            return ref[0, 0]
        n_tiles = width // LANES
        for r in range(dilation):
            for t in range(n_tiles):
                stage_ref[slot * n_tiles + t, pl.ds(r, n, stride=dilation), :] = (
                    ref[0, r, :, t * LANES:(t + 1) * LANES])
        return jnp.concatenate([stage_ref[slot * n_tiles + t] for t in range(n_tiles)], axis=1)

    lses = [token_order(ref, l_stage_ref, g) for g, ref in enumerate(l_refs)]
    outs = [token_order(ref, o_stage_ref, g) for g, ref in enumerate(o_refs)]
    m = jnp.maximum(jnp.maximum(lses[0], lses[1]), lses[2])
    es = [jnp.exp2(l - m) for l in lses]
    inv = 1.0 / (es[0] + es[1] + es[2])
    r = lax.broadcasted_iota(jnp.int32, (LANES, d), 0)
    c = lax.broadcasted_iota(jnp.int32, (LANES, d), 1) // HEAD_DIM
    expand = jnp.where(r == c, 1.0, 0.0).astype(BF16)
    merged = outs[-1]
    for e, o in zip(es[:-1], outs[:-1]):
        alpha = e * inv
        hi = alpha.astype(BF16)
        lo = (alpha - hi.astype(F32)).astype(BF16)
        merged = merged + (_dot(hi, expand) + _dot(lo, expand)) * (o - outs[-1])
    return merged


def kernel(x, c, positions, mod_w, mod_b, norm_g, ffn_w_gate, ffn_w_up, ffn_w_down,
           fox_w_in, fox_b_f, fox_q_g, fox_k_g, fox_w_out,
           dil_w_in, dil_q_g, dil_k_g, dil_w_out):
    b, s, d = x.shape
    depth = mod_w.shape[0]
    mod = _modulation(c, mod_w, mod_b).reshape(depth, b, N_SUBLAYERS, 3, 1, d)
    wg = ffn_w_gate.astype(BF16)
    wu = ffn_w_up.astype(BF16)
    wd = ffn_w_down.astype(BF16)
    cos, sin = _rope_tables(positions)

    for i in range(depth):
        shift = lambda sub: mod[i, :, sub, 0]
        scale = lambda sub: mod[i, :, sub, 1]
        gate = lambda sub: mod[i, :, sub, 2]
        g = lambda sub: norm_g[i, sub].reshape(1, d)

        x = _ffn(x, g(0), shift(0), scale(0), gate(0), wg, wu, wd, i, 0)
        j = i // 2
        if i % 2 == 0:
            q, ka, vt = _fox_in(x, g(1), shift(1), scale(1), fox_w_in[j], fox_b_f[j],
                                fox_q_g[j], fox_k_g[j])
            mixer = (_fox_attn(q, ka, vt), fox_w_out[j], gate(1))
        else:
            qkv = _dil_in(x, g(1), shift(1), scale(1), dil_w_in[j], dil_q_g[j], dil_k_g[j],
                          cos, sin)
            outs, lses = [], []
            for grp in range(len(DIL_CONFIGS)):
                o, lse = _dil_attn(qkv[3 * grp], qkv[3 * grp + 1], qkv[3 * grp + 2])
                outs.append(o)
                lses.append(lse)
            mixer = (outs, lses, dil_w_out[j], gate(1))
        x = _ffn(x, g(2), shift(2), scale(2), gate(2), wg, wu, wd, i, 1, mixer=mixer)
    return x
```

```python
import functools

import jax
import jax.numpy as jnp
from jax import lax
from jax.experimental import pallas as pl
from jax.experimental.pallas import tpu as pltpu

HEAD_DIM = 64
ROPE_DIM = 16
ROPE_THETA = 500000.0
EPS = 1e-6
MACARON_WEIGHT = 0.5
N_SUBLAYERS = 3
DIL_CONFIGS = ((128, 1), (512, 4), (2048, 16))
DIL_WINDOW_KEYS = 128

LANES = 128
MXU_DIM = 256
VMEM_LIMIT = 56 * 1024 * 1024

F32 = jnp.float32
BF16 = jnp.bfloat16


def _params(semantics, vmem=VMEM_LIMIT):
    return pltpu.CompilerParams(dimension_semantics=semantics, vmem_limit_bytes=vmem)


def _resident(block_shape, index_map):
    return pl.BlockSpec(block_shape, index_map, pipeline_mode=pl.Buffered(1))


def _dot(a, b):
    return jnp.dot(a, b, preferred_element_type=F32)


def _dot_nt(a, b):
    return lax.dot_general(a, b, (((1,), (1,)), ((), ())), preferred_element_type=F32)


def _sigmoid(x):
    return 1.0 / (1.0 + jnp.exp(-x))


def _norm_mod(x, g, shift, scale):
    ms = jnp.mean(x * x, axis=-1, keepdims=True)
    y = x * lax.rsqrt(ms + EPS) * g
    return y * (1.0 + scale) + shift


def _head_block_ones(n):
    r = lax.broadcasted_iota(jnp.int32, (n, n), 0) // HEAD_DIM
    c = lax.broadcasted_iota(jnp.int32, (n, n), 1) // HEAD_DIM
    return jnp.where(r == c, 1.0, 0.0).astype(BF16)


def _head_rms_norm(x, gain):
    ones = _head_block_ones(MXU_DIM)
    outs = []
    for c in range(x.shape[1] // MXU_DIM):
        xc = x[:, c * MXU_DIM:(c + 1) * MXU_DIM]
        ssq = _dot((xc * xc).astype(BF16), ones)
        r = lax.rsqrt(ssq * (1.0 / HEAD_DIM) + EPS)
        outs.append(xc * r * gain[:, c * MXU_DIM:(c + 1) * MXU_DIM])
    return jnp.concatenate(outs, axis=1)


def _mod_kernel(c_ref, w_ref, b_ref, o_ref):
    c = c_ref[...]
    c_act = c * _sigmoid(c)
    o_ref[0] = jnp.dot(c_act, w_ref[0], preferred_element_type=F32,
                       precision=lax.Precision.HIGHEST) + b_ref[0]


def _modulation(c, mod_w, mod_b):
    depth, d, n = mod_w.shape
    b = c.shape[0]
    tn = 1024
    return pl.pallas_call(
        _mod_kernel,
        grid=(depth, n // tn),
        in_specs=[
            pl.BlockSpec((b, d), lambda i, j: (0, 0)),
            pl.BlockSpec((1, d, tn), lambda i, j: (i, 0, j)),
            pl.BlockSpec((1, 1, tn), lambda i, j: (i, 0, j)),
        ],
        out_specs=pl.BlockSpec((1, b, tn), lambda i, j: (i, 0, j)),
        out_shape=jax.ShapeDtypeStruct((depth, b, n), F32),
        compiler_params=_params(("arbitrary", "arbitrary")),
        name="modulation",
    )(c, mod_w, mod_b.reshape(depth, 1, n))


FFN_CHUNK = 256


def _swiglu_residual(x, g_ref, sh_ref, sc_ref, gt_ref, wg_ref, wu_ref, wd_ref):
    h = _norm_mod(x, g_ref[...], sh_ref[0], sc_ref[0]).astype(BF16)
    acc = jnp.zeros_like(x)
    for c in range(wg_ref.shape[1] // FFN_CHUNK):
        cs = slice(c * FFN_CHUNK, (c + 1) * FFN_CHUNK)
        gg = _dot(h, wg_ref[:, cs])
        uu = _dot(h, wu_ref[:, cs])
        a = (gg * _sigmoid(gg) * uu).astype(BF16)
        acc = acc + _dot(a, wd_ref[cs, :])
    return x + (MACARON_WEIGHT * gt_ref[0]) * acc


def _ffn_kernel(x_ref, *refs):
    ffn_refs, o_ref = refs[:-1], refs[-1]
    o_ref[0] = _swiglu_residual(x_ref[0], *ffn_refs)


def _proj_ffn_kernel(x_ref, a_ref, wo_ref, gm_ref, *refs):
    ffn_refs, o_ref = refs[:-1], refs[-1]
    x = x_ref[0] + gm_ref[0] * _dot(a_ref[0], wo_ref[...])
    o_ref[0] = _swiglu_residual(x, *ffn_refs)


def _dil_ffn_kernel(x_ref, o0_ref, o1_ref, o2_ref, l0_ref, l1_ref, l2_ref, wo_ref, gm_ref, *refs):
    ffn_refs, o_ref, o_stage_ref, l_stage_ref = refs[:-3], refs[-3], refs[-2], refs[-1]
    merged = _dil_merge((o0_ref, o1_ref, o2_ref), (l0_ref, l1_ref, l2_ref),
                        o_stage_ref, l_stage_ref)
    x = x_ref[0] + gm_ref[0] * _dot(merged.astype(BF16), wo_ref[...])
    o_ref[0] = _swiglu_residual(x, *ffn_refs)


def _ffn(x, g, shift, scale, gate, wg, wu, wd, layer, which, mixer=None):
    b, s, d = x.shape
    f = wg.shape[-1]
    tm = 512 if mixer is not None and len(mixer) == 4 else 1024
    row = pl.BlockSpec((1, 1, d), lambda i, j: (i, 0, 0))
    act = pl.BlockSpec((1, tm, d), lambda i, j: (i, j, 0))
    ffn_specs = [
        pl.BlockSpec((1, d), lambda i, j: (0, 0)),
        row, row, row,
        _resident((None, None, d, f), lambda i, j: (layer, which, 0, 0)),
        _resident((None, None, d, f), lambda i, j: (layer, which, 0, 0)),
        _resident((None, None, f, d), lambda i, j: (layer, which, 0, 0)),
    ]
    ffn_args = (g, shift, scale, gate, wg, wu, wd)
    scratch = []
    if mixer is None:
        body, mix_specs, mix_args = _ffn_kernel, [], ()
    elif len(mixer) == 3:
        o, w_out, gate_m = mixer
        body = _proj_ffn_kernel
        mix_specs = [act, _resident(w_out.shape, lambda i, j: (0, 0)), row]
        mix_args = (o, w_out.astype(BF16), gate_m)
    else:
        outs, lses, w_out, gate_m = mixer
        body = _dil_ffn_kernel
        grouped = lambda a: pl.BlockSpec((1, a.shape[1], tm // a.shape[1], a.shape[3]),
                                         lambda i, j: (i, 0, j, 0))
        mix_specs = ([grouped(a) for a in outs] + [grouped(a) for a in lses]
                     + [_resident(w_out.shape, lambda i, j: (0, 0)), row])
        mix_args = (*outs, *lses, w_out.astype(BF16), gate_m)
        scratch = [pltpu.VMEM((len(outs) * d // LANES, tm, LANES), F32),
                   pltpu.VMEM((len(lses), tm, LANES), F32)]
    return pl.pallas_call(
        body,
        grid=(b, s // tm),
        in_specs=[act] + mix_specs + ffn_specs,
        out_specs=act,
        out_shape=jax.ShapeDtypeStruct(x.shape, x.dtype),
        scratch_shapes=scratch,
        compiler_params=_params(("arbitrary", "arbitrary")),
        name="ffn",
    )(x, *mix_args, *ffn_args)


def _cumsum_rows(x):
    n = x.shape[0]
    row = lax.broadcasted_iota(jnp.int32, x.shape, 0)
    shift = 1
    while shift < n:
        x = x + jnp.where(row >= shift, pltpu.roll(x, shift, 0), 0.0)
        shift *= 2
    return x


N_SPLIT = 3
LOG2E = 1.4426950408889634


def _fox_in_kernel(x_ref, g_ref, sh_ref, sc_ref, wqk_ref, wvt_ref, wf_ref, bf_ref, qg_ref,
                   kg_ref, sel_ref, q_ref, ka_ref, vt_ref, carry_ref, *, n_heads):
    d = x_ref.shape[2]
    tm = x_ref.shape[1]

    @pl.when(pl.program_id(1) == 0)
    def _():
        carry_ref[...] = jnp.zeros_like(carry_ref)

    h = _norm_mod(x_ref[0], g_ref[...], sh_ref[0], sc_ref[0]).astype(BF16)
    z = _dot(h, wf_ref[...]) + bf_ref[...]
    logf = jnp.minimum(z, 0.0) - jnp.log1p(jnp.exp(-jnp.abs(z)))
    lane = lax.broadcasted_iota(jnp.int32, logf.shape, 1)
    logf = jnp.where(lane < n_heads, logf, 0.0)
    cum = _cumsum_rows(logf) + carry_ref[...]
    carry_ref[...] = cum[tm - 1:tm, :]

    q = _head_rms_norm(_dot(h, wqk_ref[:, 0:d]), qg_ref[...])
    q_ref[0] = (q * (LOG2E * HEAD_DIM ** -0.5)).astype(BF16)
    k = _head_rms_norm(_dot(h, wqk_ref[:, d:2 * d]), kg_ref[...]).astype(BF16)
    vt_ref[0] = _dot_nt(wvt_ref[...], h).astype(BF16)

    terms = []
    rest = cum * LOG2E
    for _ in range(N_SPLIT):
        t = rest.astype(BF16)
        terms.append(t)
        rest = rest - t.astype(F32)
    kx = _dot(jnp.concatenate(terms, axis=1), sel_ref[...]).astype(BF16)
    pieces = []
    for p in range(d // LANES):
        pieces += [k[:, p * LANES:(p + 1) * LANES], kx[:, p * LANES:(p + 1) * LANES]]
    ka_ref[0] = jnp.concatenate(pieces, axis=1)


def _fox_in(x, g, shift, scale, w_in, b_f, q_g, k_g, tm=1024):
    b, s, d = x.shape
    n_heads = d // HEAD_DIM
    wqk = w_in[:, :2 * d].astype(BF16)
    wvt = w_in[:, 2 * d:3 * d].T.astype(BF16)
    wf = jnp.pad(w_in[:, 3 * d:], ((0, 0), (0, LANES - n_heads))).astype(BF16)
    bf = jnp.pad(b_f, (0, LANES - n_heads)).reshape(1, LANES)
    qg = jnp.tile(q_g, n_heads).reshape(1, d)
    kg = jnp.tile(k_g, n_heads).reshape(1, d)
    heads = jnp.arange(n_heads)
    sel = jnp.zeros((N_SPLIT * LANES, d), BF16)
    for t in range(N_SPLIT):
        sel = sel.at[t * LANES + heads, (heads // 2) * LANES + (heads % 2) * N_SPLIT + t].set(1.0)
    row = pl.BlockSpec((1, 1, d), lambda i, j: (i, 0, 0))
    full = lambda shp: pl.BlockSpec(shp, lambda i, j: (0, 0))
    res = lambda a: _resident(a.shape, lambda i, j: (0, 0))
    act = pl.BlockSpec((1, tm, d), lambda i, j: (i, j, 0))
    return pl.pallas_call(
        functools.partial(_fox_in_kernel, n_heads=n_heads),
        grid=(b, s // tm),
        in_specs=[act, full((1, d)), row, row, res(wqk), res(wvt), res(wf),
                  full((1, LANES)), full((1, d)), full((1, d)), res(sel)],
        out_specs=[act, pl.BlockSpec((1, tm, 2 * d), lambda i, j: (i, j, 0)),
                   pl.BlockSpec((1, d, tm), lambda i, j: (i, 0, j))],
        out_shape=[jax.ShapeDtypeStruct((b, s, d), BF16),
                   jax.ShapeDtypeStruct((b, s, 2 * d), BF16),
                   jax.ShapeDtypeStruct((b, d, s), BF16)],
        scratch_shapes=[pltpu.VMEM((1, LANES), F32)],
        compiler_params=_params(("arbitrary", "arbitrary")),
        name="fox_in",
    )(x, g, shift, scale, wqk, wvt, wf, bf, qg, kg, sel)


ONES_ROWS = 16
STEPS_PER_BLOCK = 4


def _fox_attn_kernel(q_ref, ka_ref, vt_ref, o_ref, qa_ref, m_ref, acc_ref, s_ref, *, tq, strip):
    tk = tq
    n_tiles = q_ref.shape[1] // tq
    lane = lax.broadcasted_iota(jnp.int32, (1, LANES), 1)
    q2 = q_ref[0]
    zero = jnp.zeros_like(q2)
    for hh in range(2):
        head = (lane < HEAD_DIM) if hh == 0 else (lane >= HEAD_DIM)
        pick = jnp.logical_and(lane >= hh * N_SPLIT, lane < (hh + 1) * N_SPLIT)
        minus_one = jnp.broadcast_to(jnp.where(pick, -1.0, 0.0), q2.shape).astype(BF16)
        qa_ref[hh] = jnp.concatenate([jnp.where(head, q2, zero), minus_one], axis=1)

    ones_rows = jnp.ones((ONES_ROWS, tk), BF16)
    chains = [(hh, slice(st * strip, (st + 1) * strip), st)
              for hh in range(2) for st in range(tq // strip)]
    n = len(chains)

    def n_keys(c, i, j):
        return (chains[c][2] + 1) * strip if i == j else tk

    def logits(c, i, j):
        hh, _, st = chains[c]
        keys = slice(j * tk, j * tk + n_keys(c, i, j))
        qrys = slice(i * tq + st * strip, i * tq + (st + 1) * strip)
        s = _dot_nt(ka_ref[0, keys, :], qa_ref[hh, qrys, :])
        if i == j:
            key = lax.broadcasted_iota(jnp.int32, s.shape, 0)
            qry = lax.broadcasted_iota(jnp.int32, s.shape, 1) + st * strip
            s = jnp.where(key <= qry, s, -jnp.inf)
        return s

    def softmax_pv(c, i, j):
        hh, cols, _ = chains[c]
        nk = n_keys(c, i, j)
        s = s_ref[c, :nk]
        m_prev = m_ref[hh, :, cols]
        m_new = jnp.maximum(m_prev, jnp.max(s, axis=0, keepdims=True))
        p = jnp.exp2(s - m_new)
        alpha = jnp.exp2(m_prev - m_new)
        m_ref[hh, :, cols] = m_new
        vt_ones = jnp.concatenate([vt_ref[0, :, j * tk:j * tk + nk], ones_rows[:, :nk]], axis=0)
        acc_ref[hh, :, cols] = alpha * acc_ref[hh, :, cols] + _dot(vt_ones, p.astype(BF16))

    def step(i, j, nxt):
        if j == 0:
            m_ref[...] = jnp.full_like(m_ref, -jnp.inf)
            acc_ref[...] = jnp.zeros_like(acc_ref)
        ahead = 2
        s_next = []
        for c in range(n + ahead):
            if nxt is not None and c < n:
                s_next.append(logits(c, *nxt))
            if c >= ahead:
                softmax_pv(c - ahead, i, j)
                if nxt is not None:
                    s_ref[c - ahead, :n_keys(c - ahead, *nxt)] = s_next[c - ahead]
        if j == i:
            dim = lax.broadcasted_iota(jnp.int32, (LANES, 1), 0)
            o_heads = [acc_ref[hh, :LANES, :] * (1.0 / acc_ref[hh, LANES:LANES + 1, :])
                       for hh in range(2)]
            o_ref[0, i * tq:(i + 1) * tq, :] = (
                jnp.where(dim < HEAD_DIM, o_heads[0], o_heads[1]).T.astype(o_ref.dtype))

    for c in range(n):
        s_ref[c, :n_keys(c, 0, 0)] = logits(c, 0, 0)

    sequence = [(i, j) for i in range(n_tiles) for j in range(i + 1)]
    always = pl.program_id(0) >= 0
    for start in range(0, len(sequence), STEPS_PER_BLOCK):

        @pl.when(always)
        def _(start=start):
            for idx in range(start, min(start + STEPS_PER_BLOCK, len(sequence))):
                nxt = sequence[idx + 1] if idx + 1 < len(sequence) else None
                step(*sequence[idx], nxt)


def _fox_attn(q, ka, vt, tq=512):
    b, s, d = q.shape
    qo = pl.BlockSpec((1, s, LANES), lambda bi, p: (bi, 0, p))
    return pl.pallas_call(
        functools.partial(_fox_attn_kernel, tq=tq, strip=MXU_DIM),
        grid=(b, d // LANES),
        in_specs=[qo, pl.BlockSpec((1, s, 2 * LANES), lambda bi, p: (bi, 0, p)),
                  pl.BlockSpec((1, LANES, s), lambda bi, p: (bi, p, 0))],
        out_specs=qo,
        out_shape=jax.ShapeDtypeStruct((b, s, d), BF16),
        scratch_shapes=[pltpu.VMEM((2, s, 2 * LANES), BF16),
                        pltpu.VMEM((2, 1, tq), F32),
                        pltpu.VMEM((2, LANES + ONES_ROWS, tq), F32),
                        pltpu.VMEM((2 * tq // MXU_DIM, tq, MXU_DIM), F32)],
        compiler_params=_params(("arbitrary", "arbitrary")),
        name="fox_attn",
    )(q, ka, vt)


def _rope_kernel(pos_ref, invf_ref, cos_ref, sin_ref):
    ang = pos_ref[0].astype(F32) * invf_ref[...]
    d = lax.broadcasted_iota(jnp.int32, ang.shape, 1) % HEAD_DIM
    half = ROPE_DIM // 2
    cos_ref[0] = jnp.where(d < ROPE_DIM, jnp.cos(ang), 1.0)
    sn = jnp.sin(ang)
    sin_ref[0] = jnp.where(d < half, -sn, jnp.where(d < ROPE_DIM, sn, 0.0))


def _rope_tables(positions, tm=512):
    b, s = positions.shape
    half = ROPE_DIM // 2
    inv_freq = ROPE_THETA ** (-(jnp.arange(half, dtype=F32) * 2.0 / ROPE_DIM))
    d = jnp.arange(LANES) % HEAD_DIM
    invf = jnp.where(d < ROPE_DIM, inv_freq[d % half], 0.0).reshape(1, LANES)
    tab = pl.BlockSpec((1, tm, LANES), lambda i, j: (i, j, 0))
    return pl.pallas_call(
        _rope_kernel,
        grid=(b, s // tm),
        in_specs=[pl.BlockSpec((1, tm, 1), lambda i, j: (i, j, 0)),
                  pl.BlockSpec((1, LANES), lambda i, j: (0, 0))],
        out_specs=[tab, tab],
        out_shape=[jax.ShapeDtypeStruct((b, s, LANES), F32)] * 2,
        compiler_params=_params(("arbitrary", "arbitrary")),
        name="rope_tables",
    )(positions.reshape(b, s, 1), invf)


def _rope(x, cos, sin):
    half = ROPE_DIM // 2
    d = lax.broadcasted_iota(jnp.int32, (1, LANES), 1) % HEAD_DIM
    outs = []
    for c in range(x.shape[1] // LANES):
        xc = x[:, c * LANES:(c + 1) * LANES]
        partner = jnp.where(d < half, pltpu.roll(xc, LANES - half, 1), pltpu.roll(xc, half, 1))
        outs.append(xc * cos + partner * sin)
    return jnp.concatenate(outs, axis=1)


def _dil_in_kernel(x_ref, g_ref, sh_ref, sc_ref, w_ref, qg_ref, kg_ref, cos_ref, sin_ref,
                   *refs):
    out_refs = refs
    tm, d = x_ref.shape[1:]
    h = _norm_mod(x_ref[0], g_ref[...], sh_ref[0], sc_ref[0]).astype(BF16)
    new = lax.broadcasted_iota(jnp.int32, (tm, tm), 0)
    old = lax.broadcasted_iota(jnp.int32, (tm, tm), 1)
    for grp, (_, dilation) in enumerate(DIL_CONFIGS):
        n = tm // dilation
        if dilation == 1:
            h_g, cos, sin = h, cos_ref[0], sin_ref[0]
        else:
            perm = jnp.where(old == (new % n) * dilation + new // n, 1.0, 0.0).astype(BF16)
            h_g = _dot(perm, h).astype(BF16)
            strided = lambda ref: jnp.concatenate(
                [ref[0, pl.ds(r, n, stride=dilation), :] for r in range(dilation)], axis=0)
            cos, sin = strided(cos_ref), strided(sin_ref)
        base = grp * 3 * d
        q = _head_rms_norm(_dot(h_g, w_ref[:, base:base + d]), qg_ref[grp:grp + 1, :])
        k = _head_rms_norm(_dot(h_g, w_ref[:, base + d:base + 2 * d]), kg_ref[grp:grp + 1, :])
        vals = (_rope(q, cos, sin) * (LOG2E * HEAD_DIM ** -0.5), _rope(k, cos, sin),
                _dot(h_g, w_ref[:, base + 2 * d:base + 3 * d]))
        for out_ref, val in zip(out_refs[3 * grp:3 * grp + 3], vals):
            val = val.astype(BF16)
            for r in range(dilation):
                out_ref[0, r] = val[r * n:(r + 1) * n]


def _dil_in(x, g, shift, scale, w_in, q_g, k_g, cos, sin, tm=256):
    b, s, d = x.shape
    n_groups = len(DIL_CONFIGS)
    n_heads = d // HEAD_DIM
    qg = jnp.tile(q_g, (1, n_heads))
    kg = jnp.tile(k_g, (1, n_heads))
    row = pl.BlockSpec((1, 1, d), lambda i, j: (i, 0, 0))
    full = lambda shp: pl.BlockSpec(shp, lambda i, j: (0, 0))
    act = pl.BlockSpec((1, tm, d), lambda i, j: (i, j, 0))
    tab = pl.BlockSpec((1, tm, LANES), lambda i, j: (i, j, 0))
    w = w_in.astype(BF16)
    out_specs, out_shape = [], []
    for _, dil in DIL_CONFIGS:
        out_specs += [pl.BlockSpec((1, dil, tm // dil, d), lambda i, j: (i, 0, j, 0))] * 3
        out_shape += [jax.ShapeDtypeStruct((b, dil, s // dil, d), BF16)] * 3
    return pl.pallas_call(
        _dil_in_kernel,
        grid=(b, s // tm),
        in_specs=[act, full((1, d)), row, row, _resident(w.shape, lambda i, j: (0, 0)),
                  full((n_groups, d)), full((n_groups, d)), tab, tab],
        out_specs=out_specs,
        out_shape=out_shape,
        compiler_params=_params(("arbitrary", "arbitrary")),
        name="dil_in",
    )(x, g, shift, scale, w, qg, kg, cos, sin)


def _dil_attn_kernel(q_ref, kc_ref, kp_ref, vc_ref, vp_ref, o_ref, lse_ref, lse_t_ref):
    w = DIL_WINDOW_KEYS
    n_res, tq, d = q_ref.shape[1:]
    i = pl.program_id(2)
    lane = lax.broadcasted_iota(jnp.int32, (1, LANES), 1)
    key = lax.broadcasted_iota(jnp.int32, (2 * w, 2 * w), 0)
    qry = lax.broadcasted_iota(jnp.int32, (2 * w, 2 * w), 1) % w
    band = jnp.logical_and(key >= qry, key <= qry + w)
    bias_band = jnp.where(band, 0.0, -jnp.inf)
    in_seq = jnp.logical_or(key >= w, i > 0)
    bias_first = jnp.where(jnp.logical_and(band, in_seq), 0.0, -jnp.inf)
    dim = lax.broadcasted_iota(jnp.int32, (LANES, 1), 0)

    chains = [(res, p, sub) for res in range(n_res) for p in range(d // LANES)
              for sub in range(tq // w)]
    head_a = jnp.where(lane < HEAD_DIM, 1.0, 0.0).astype(BF16)
    head_b = jnp.where(lane >= HEAD_DIM, 1.0, 0.0).astype(BF16)
    ones_rows = jnp.ones((ONES_ROWS, 2 * w), BF16)

    def window(cur_ref, prev_ref, res, p, sub):
        cs = slice(p * LANES, (p + 1) * LANES)
        if sub == 0:
            return jnp.concatenate([prev_ref[0, res, :, cs], cur_ref[0, res, :w, cs]], axis=0)
        return cur_ref[0, res, (sub - 1) * w:(sub + 1) * w, cs]

    def logits(c):
        res, p, sub = chains[c]
        qs = q_ref[0, res, sub * w:(sub + 1) * w, p * LANES:(p + 1) * LANES]
        q_both = jnp.concatenate([qs * head_a, qs * head_b], axis=0)
        s = _dot_nt(window(kc_ref, kp_ref, res, p, sub), q_both)
        return s + (bias_first if sub == 0 else bias_band)

    def finish(c, s):
        res, p, sub = chains[c]
        rows = slice(sub * w, (sub + 1) * w)
        m = jnp.max(s, axis=0, keepdims=True)
        e = jnp.exp2(s - m)
        v_t = jnp.concatenate([window(vc_ref, vp_ref, res, p, sub).T, ones_rows], axis=0)
        o_l = _dot(v_t, e.astype(BF16))
        l = o_l[LANES:LANES + 1]
        o_t = o_l[:LANES] * (1.0 / l)
        o_ref[0, res, rows, p * LANES:(p + 1) * LANES] = (
            jnp.where(dim < HEAD_DIM, o_t[:, :w], o_t[:, w:]).T)
        lse = m + jnp.log(l) * LOG2E
        lse_t_ref[res, 2 * p:2 * p + 1, rows] = lse[:, :w]
        lse_t_ref[res, 2 * p + 1:2 * p + 2, rows] = lse[:, w:]

    lse_t_ref[...] = jnp.zeros_like(lse_t_ref)
    ahead = 4
    pending = [logits(c) for c in range(ahead)]
    for c in range(len(chains)):
        if c + ahead < len(chains):
            pending.append(logits(c + ahead))
        finish(c, pending[c])
    for res in range(n_res):
        lse_ref[0, res] = lse_t_ref[res].T


def _dil_attn(q, k, v, rows_per_step=512):
    b, dilation, sub_len, d = q.shape
    w = DIL_WINDOW_KEYS
    tq = min(rows_per_step, sub_len)
    n_res = min(rows_per_step // tq, dilation)
    cur = pl.BlockSpec((1, n_res, tq, d), lambda bi, r, i: (bi, r, i, 0))
    prev = pl.BlockSpec((1, n_res, w, d),
                        lambda bi, r, i: (bi, r, jnp.maximum(i * (tq // w) - 1, 0), 0))
    return pl.pallas_call(
        _dil_attn_kernel,
        grid=(b, dilation // n_res, sub_len // tq),
        in_specs=[cur, cur, prev, cur, prev],
        out_specs=[cur, pl.BlockSpec((1, n_res, tq, LANES), lambda bi, r, i: (bi, r, i, 0))],
        out_shape=[jax.ShapeDtypeStruct(q.shape, F32),
                   jax.ShapeDtypeStruct((b, dilation, sub_len, LANES), F32)],
        scratch_shapes=[pltpu.VMEM((n_res, LANES, tq), F32)],
        compiler_params=_params(("arbitrary", "arbitrary", "arbitrary")),
        name="dil_attn",
    )(q, k, k, v, v)


def _dil_merge(o_refs, l_refs, o_stage_ref, l_stage_ref):
    d = o_refs[0].shape[3]

    def token_order(ref, stage_ref, slot):
        dilation, n, width = ref.shape[1:]
        if dilation == 1:
            return ref[0, 0]
        n_tiles = width // LANES
        for r in range(dilation):
            for t in range(n_tiles):
                stage_ref[slot * n_tiles + t, pl.ds(r, n, stride=dilation), :] = (
                    ref[0, r, :, t * LANES:(t + 1) * LANES])
        return jnp.concatenate([stage_ref[slot * n_tiles + t] for t in range(n_tiles)], axis=1)

    lses = [token_order(ref, l_stage_ref, g) for g, ref in enumerate(l_refs)]
    outs = [token_order(ref, o_stage_ref, g) for g, ref in enumerate(o_refs)]
    m = jnp.maximum(jnp.maximum(lses[0], lses[1]), lses[2])
    es = [jnp.exp2(l - m) for l in lses]
    inv = 1.0 / (es[0] + es[1] + es[2])
    r = lax.broadcasted_iota(jnp.int32, (LANES, d), 0)
    c = lax.broadcasted_iota(jnp.int32, (LANES, d), 1) // HEAD_DIM
    expand = jnp.where(r == c, 1.0, 0.0).astype(BF16)
    merged = outs[-1]
    for e, o in zip(es[:-1], outs[:-1]):
        alpha = e * inv
        hi = alpha.astype(BF16)
        lo = (alpha - hi.astype(F32)).astype(BF16)
        merged = merged + (_dot(hi, expand) + _dot(lo, expand)) * (o - outs[-1])
    return merged


def kernel(x, c, positions, mod_w, mod_b, norm_g, ffn_w_gate, ffn_w_up, ffn_w_down,
           fox_w_in, fox_b_f, fox_q_g, fox_k_g, fox_w_out,
           dil_w_in, dil_q_g, dil_k_g, dil_w_out):
    b, s, d = x.shape
    depth = mod_w.shape[0]
    mod = _modulation(c, mod_w, mod_b).reshape(depth, b, N_SUBLAYERS, 3, 1, d)
    wg = ffn_w_gate.astype(BF16)
    wu = ffn_w_up.astype(BF16)
    wd = ffn_w_down.astype(BF16)
    cos, sin = _rope_tables(positions)

    for i in range(depth):
        shift = lambda sub: mod[i, :, sub, 0]
        scale = lambda sub: mod[i, :, sub, 1]
        gate = lambda sub: mod[i, :, sub, 2]
        g = lambda sub: norm_g[i, sub].reshape(1, d)

        x = _ffn(x, g(0), shift(0), scale(0), gate(0), wg, wu, wd, i, 0)
        j = i // 2
        if i % 2 == 0:
            q, ka, vt = _fox_in(x, g(1), shift(1), scale(1), fox_w_in[j], fox_b_f[j],
                                fox_q_g[j], fox_k_g[j])
            mixer = (_fox_attn(q, ka, vt), fox_w_out[j], gate(1))
        else:
            qkv = _dil_in(x, g(1), shift(1), scale(1), dil_w_in[j], dil_q_g[j], dil_k_g[j],
                          cos, sin)
            outs, lses = [], []
            for grp in range(len(DIL_CONFIGS)):
                o, lse = _dil_attn(qkv[3 * grp], qkv[3 * grp + 1], qkv[3 * grp + 2])
                outs.append(o)
                lses.append(lse)
            mixer = (outs, lses, dil_w_out[j], gate(1))
        x = _ffn(x, g(2), shift(2), scale(2), gate(2), wg, wu, wd, i, 1, mixer=mixer)
    return x
```

```python
import functools

import jax
import jax.numpy as jnp
from jax import lax
from jax.experimental import pallas as pl
from jax.experimental.pallas import tpu as pltpu

HEAD_DIM = 64
ROPE_DIM = 16
ROPE_THETA = 500000.0
EPS = 1e-6
MACARON_WEIGHT = 0.5
N_SUBLAYERS = 3
DIL_CONFIGS = ((128, 1), (512, 4), (2048, 16))
DIL_WINDOW_KEYS = 128

LANES = 128
MXU_DIM = 256
VMEM_LIMIT = 56 * 1024 * 1024

F32 = jnp.float32
BF16 = jnp.bfloat16


def _params(semantics, vmem=VMEM_LIMIT):
    return pltpu.CompilerParams(dimension_semantics=semantics, vmem_limit_bytes=vmem)


def _resident(block_shape, index_map):
    return pl.BlockSpec(block_shape, index_map, pipeline_mode=pl.Buffered(1))


def _dot(a, b):
    return jnp.dot(a, b, preferred_element_type=F32)


def _dot_nt(a, b):
    return lax.dot_general(a, b, (((1,), (1,)), ((), ())), preferred_element_type=F32)


def _sigmoid(x):
    return 1.0 / (1.0 + jnp.exp(-x))


def _norm_mod(x, g, shift, scale):
    ms = jnp.mean(x * x, axis=-1, keepdims=True)
    y = x * lax.rsqrt(ms + EPS) * g
    return y * (1.0 + scale) + shift


def _head_block_ones(n):
    r = lax.broadcasted_iota(jnp.int32, (n, n), 0) // HEAD_DIM
    c = lax.broadcasted_iota(jnp.int32, (n, n), 1) // HEAD_DIM
    return jnp.where(r == c, 1.0, 0.0).astype(BF16)


def _head_rms_norm(x, gain):
    ones = _head_block_ones(MXU_DIM)
    outs = []
    for c in range(x.shape[1] // MXU_DIM):
        xc = x[:, c * MXU_DIM:(c + 1) * MXU_DIM]
        ssq = _dot((xc * xc).astype(BF16), ones)
        r = lax.rsqrt(ssq * (1.0 / HEAD_DIM) + EPS)
        outs.append(xc * r * gain[:, c * MXU_DIM:(c + 1) * MXU_DIM])
    return jnp.concatenate(outs, axis=1)


def _mod_kernel(c_ref, w_ref, b_ref, o_ref):
    c = c_ref[...]
    c_act = c * _sigmoid(c)
    o_ref[0] = jnp.dot(c_act, w_ref[0], preferred_element_type=F32,
                       precision=lax.Precision.HIGHEST) + b_ref[0]


def _modulation(c, mod_w, mod_b):
    depth, d, n = mod_w.shape
    b = c.shape[0]
    tn = 1024
    return pl.pallas_call(
        _mod_kernel,
        grid=(depth, n // tn),
        in_specs=[
            pl.BlockSpec((b, d), lambda i, j: (0, 0)),
            pl.BlockSpec((1, d, tn), lambda i, j: (i, 0, j)),
            pl.BlockSpec((1, 1, tn), lambda i, j: (i, 0, j)),
        ],
        out_specs=pl.BlockSpec((1, b, tn), lambda i, j: (i, 0, j)),
        out_shape=jax.ShapeDtypeStruct((depth, b, n), F32),
        compiler_params=_params(("arbitrary", "arbitrary")),
        name="modulation",
    )(c, mod_w, mod_b.reshape(depth, 1, n))


FFN_CHUNK = 256


def _swiglu_residual(x, g_ref, sh_ref, sc_ref, gt_ref, wg_ref, wu_ref, wd_ref):
    h = _norm_mod(x, g_ref[...], sh_ref[0], sc_ref[0]).astype(BF16)
    acc = jnp.zeros_like(x)
    for c in range(wg_ref.shape[1] // FFN_CHUNK):
        cs = slice(c * FFN_CHUNK, (c + 1) * FFN_CHUNK)
        gg = _dot(h, wg_ref[:, cs])
        uu = _dot(h, wu_ref[:, cs])
        a = (gg * _sigmoid(gg) * uu).astype(BF16)
        acc = acc + _dot(a, wd_ref[cs, :])
    return x + (MACARON_WEIGHT * gt_ref[0]) * acc


def _ffn_kernel(x_ref, *refs):
    ffn_refs, o_ref = refs[:-1], refs[-1]
    o_ref[0] = _swiglu_residual(x_ref[0], *ffn_refs)


def _proj_ffn_kernel(x_ref, a_ref, wo_ref, gm_ref, *refs):
    ffn_refs, o_ref = refs[:-1], refs[-1]
    x = x_ref[0] + gm_ref[0] * _dot(a_ref[0], wo_ref[...])
    o_ref[0] = _swiglu_residual(x, *ffn_refs)


def _dil_ffn_kernel(x_ref, o0_ref, o1_ref, o2_ref, l0_ref, l1_ref, l2_ref, wo_ref, gm_ref, *refs):
    ffn_refs, o_ref, o_stage_ref, l_stage_ref = refs[:-3], refs[-3], refs[-2], refs[-1]
    merged = _dil_merge((o0_ref, o1_ref, o2_ref), (l0_ref, l1_ref, l2_ref),
                        o_stage_ref, l_stage_ref)
    x = x_ref[0] + gm_ref[0] * _dot(merged.astype(BF16), wo_ref[...])
    o_ref[0] = _swiglu_residual(x, *ffn_refs)


def _ffn(x, g, shift, scale, gate, wg, wu, wd, layer, which, mixer=None):
    b, s, d = x.shape
    f = wg.shape[-1]
    tm = 512 if mixer is not None and len(mixer) == 4 else 1024
    row = pl.BlockSpec((1, 1, d), lambda i, j: (i, 0, 0))
    act = pl.BlockSpec((1, tm, d), lambda i, j: (i, j, 0))
    ffn_specs = [
        pl.BlockSpec((1, d), lambda i, j: (0, 0)),
        row, row, row,
        _resident((None, None, d, f), lambda i, j: (layer, which, 0, 0)),
        _resident((None, None, d, f), lambda i, j: (layer, which, 0, 0)),
        _resident((None, None, f, d), lambda i, j: (layer, which, 0, 0)),
    ]
    ffn_args = (g, shift, scale, gate, wg, wu, wd)
    scratch = []
    if mixer is None:
        body, mix_specs, mix_args = _ffn_kernel, [], ()
    elif len(mixer) == 3:
        o, w_out, gate_m = mixer
        body = _proj_ffn_kernel
        mix_specs = [act, _resident(w_out.shape, lambda i, j: (0, 0)), row]
        mix_args = (o, w_out.astype(BF16), gate_m)
    else:
        outs, lses, w_out, gate_m = mixer
        body = _dil_ffn_kernel
        grouped = lambda a: pl.BlockSpec((1, a.shape[1], tm // a.shape[1], a.shape[3]),
                                         lambda i, j: (i, 0, j, 0))
        mix_specs = ([grouped(a) for a in outs] + [grouped(a) for a in lses]
                     + [_resident(w_out.shape, lambda i, j: (0, 0)), row])
        mix_args = (*outs, *lses, w_out.astype(BF16), gate_m)
        scratch = [pltpu.VMEM((len(outs) * d // LANES, tm, LANES), F32),
                   pltpu.VMEM((len(lses), tm, LANES), F32)]
    return pl.pallas_call(
        body,
        grid=(b, s // tm),
        in_specs=[act] + mix_specs + ffn_specs,
        out_specs=act,
        out_shape=jax.ShapeDtypeStruct(x.shape, x.dtype),
        scratch_shapes=scratch,
        compiler_params=_params(("arbitrary", "arbitrary")),
        name="ffn",
    )(x, *mix_args, *ffn_args)


def _cumsum_rows(x):
    n = x.shape[0]
    row = lax.broadcasted_iota(jnp.int32, x.shape, 0)
    shift = 1
    while shift < n:
        x = x + jnp.where(row >= shift, pltpu.roll(x, shift, 0), 0.0)
        shift *= 2
    return x


N_SPLIT = 3
LOG2E = 1.4426950408889634


def _fox_in_kernel(x_ref, g_ref, sh_ref, sc_ref, wqk_ref, wvt_ref, wf_ref, bf_ref, qg_ref,
                   kg_ref, sel_ref, q_ref, ka_ref, vt_ref, carry_ref, *, n_heads):
    d = x_ref.shape[2]
    tm = x_ref.shape[1]

    @pl.when(pl.program_id(1) == 0)
    def _():
        carry_ref[...] = jnp.zeros_like(carry_ref)

    h = _norm_mod(x_ref[0], g_ref[...], sh_ref[0], sc_ref[0]).astype(BF16)
    z = _dot(h, wf_ref[...]) + bf_ref[...]
    logf = jnp.minimum(z, 0.0) - jnp.log1p(jnp.exp(-jnp.abs(z)))
    lane = lax.broadcasted_iota(jnp.int32, logf.shape, 1)
    logf = jnp.where(lane < n_heads, logf, 0.0)
    cum = _cumsum_rows(logf) + carry_ref[...]
    carry_ref[...] = cum[tm - 1:tm, :]

    q = _head_rms_norm(_dot(h, wqk_ref[:, 0:d]), qg_ref[...])
    q_ref[0] = (q * (LOG2E * HEAD_DIM ** -0.5)).astype(BF16)
    k = _head_rms_norm(_dot(h, wqk_ref[:, d:2 * d]), kg_ref[...]).astype(BF16)
    vt_ref[0] = _dot_nt(wvt_ref[...], h).astype(BF16)

    terms = []
    rest = cum * LOG2E
    for _ in range(N_SPLIT):
        t = rest.astype(BF16)
        terms.append(t)
        rest = rest - t.astype(F32)
    kx = _dot(jnp.concatenate(terms, axis=1), sel_ref[...]).astype(BF16)
    pieces = []
    for p in range(d // LANES):
        pieces += [k[:, p * LANES:(p + 1) * LANES], kx[:, p * LANES:(p + 1) * LANES]]
    ka_ref[0] = jnp.concatenate(pieces, axis=1)


def _fox_in(x, g, shift, scale, w_in, b_f, q_g, k_g, tm=1024):
    b, s, d = x.shape
    n_heads = d // HEAD_DIM
    wqk = w_in[:, :2 * d].astype(BF16)
    wvt = w_in[:, 2 * d:3 * d].T.astype(BF16)
    wf = jnp.pad(w_in[:, 3 * d:], ((0, 0), (0, LANES - n_heads))).astype(BF16)
    bf = jnp.pad(b_f, (0, LANES - n_heads)).reshape(1, LANES)
    qg = jnp.tile(q_g, n_heads).reshape(1, d)
    kg = jnp.tile(k_g, n_heads).reshape(1, d)
    heads = jnp.arange(n_heads)
    sel = jnp.zeros((N_SPLIT * LANES, d), BF16)
    for t in range(N_SPLIT):
        sel = sel.at[t * LANES + heads, (heads // 2) * LANES + (heads % 2) * N_SPLIT + t].set(1.0)
    row = pl.BlockSpec((1, 1, d), lambda i, j: (i, 0, 0))
    full = lambda shp: pl.BlockSpec(shp, lambda i, j: (0, 0))
    res = lambda a: _resident(a.shape, lambda i, j: (0, 0))
    act = pl.BlockSpec((1, tm, d), lambda i, j: (i, j, 0))
    return pl.pallas_call(
        functools.partial(_fox_in_kernel, n_heads=n_heads),
        grid=(b, s // tm),
        in_specs=[act, full((1, d)), row, row, res(wqk), res(wvt), res(wf),
                  full((1, LANES)), full((1, d)), full((1, d)), res(sel)],
        out_specs=[act, pl.BlockSpec((1, tm, 2 * d), lambda i, j: (i, j, 0)),
                   pl.BlockSpec((1, d, tm), lambda i, j: (i, 0, j))],
        out_shape=[jax.ShapeDtypeStruct((b, s, d), BF16),
                   jax.ShapeDtypeStruct((b, s, 2 * d), BF16),
                   jax.ShapeDtypeStruct((b, d, s), BF16)],
        scratch_shapes=[pltpu.VMEM((1, LANES), F32)],
        compiler_params=_params(("arbitrary", "arbitrary")),
        name="fox_in",
    )(x, g, shift, scale, wqk, wvt, wf, bf, qg, kg, sel)


ONES_ROWS = 16


def _fox_attn_kernel(q_ref, ka_ref, vt_ref, o_ref, qa_ref, m_ref, acc_ref, s_ref, *, tq, strip):
    tk = tq
    n_tiles = q_ref.shape[1] // tq
    lane = lax.broadcasted_iota(jnp.int32, (1, LANES), 1)
    q2 = q_ref[0]
    zero = jnp.zeros_like(q2)
    for hh in range(2):
        head = (lane < HEAD_DIM) if hh == 0 else (lane >= HEAD_DIM)
        pick = jnp.logical_and(lane >= hh * N_SPLIT, lane < (hh + 1) * N_SPLIT)
        minus_one = jnp.broadcast_to(jnp.where(pick, -1.0, 0.0), q2.shape).astype(BF16)
        qa_ref[hh] = jnp.concatenate([jnp.where(head, q2, zero), minus_one], axis=1)

    ones_rows = jnp.ones((ONES_ROWS, tk), BF16)
    chains = [(hh, slice(st * strip, (st + 1) * strip), st)
              for st in range(tq // strip) for hh in range(2)]
    n = len(chains)

    def n_keys(c, i, j):
        return (chains[c][2] + 1) * strip if i == j else tk

    def logits(c, i, j):
        hh, _, st = chains[c]
        keys = slice(j * tk, j * tk + n_keys(c, i, j))
        qrys = slice(i * tq + st * strip, i * tq + (st + 1) * strip)
        s = _dot_nt(ka_ref[0, keys, :], qa_ref[hh, qrys, :])
        if i == j:
            key = lax.broadcasted_iota(jnp.int32, s.shape, 0)
            qry = lax.broadcasted_iota(jnp.int32, s.shape, 1) + st * strip
            s = jnp.where(key <= qry, s, -jnp.inf)
        return s

    def softmax_pv(c, i, j):
        hh, cols, _ = chains[c]
        nk = n_keys(c, i, j)
        s = s_ref[c, :nk]
        m_prev = m_ref[hh, :, cols]
        m_new = jnp.maximum(m_prev, jnp.max(s, axis=0, keepdims=True))
        p = jnp.exp2(s - m_new)
        alpha = jnp.exp2(m_prev - m_new)
        m_ref[hh, :, cols] = m_new
        v_t = vt_ref[0, hh * HEAD_DIM:(hh + 1) * HEAD_DIM, j * tk:j * tk + nk]
        vt_ones = jnp.concatenate([v_t, ones_rows[:, :nk]], axis=0)
        acc_ref[hh, :, cols] = alpha * acc_ref[hh, :, cols] + _dot(vt_ones, p.astype(BF16))

    def step(i, j, nxt):
        if j == 0:
            m_ref[...] = jnp.full_like(m_ref, -jnp.inf)
            acc_ref[...] = jnp.zeros_like(acc_ref)
        ahead = 2
        s_next = []
        for c in range(n + ahead):
            if nxt is not None and c < n:
                s_next.append(logits(c, *nxt))
            if c >= ahead:
                softmax_pv(c - ahead, i, j)
                if nxt is not None:
                    s_ref[c - ahead, :n_keys(c - ahead, *nxt)] = s_next[c - ahead]
        if j == i:
            o_heads = [acc_ref[hh, :HEAD_DIM, :] * (1.0 / acc_ref[hh, HEAD_DIM:HEAD_DIM + 1, :])
                       for hh in range(2)]
            o_ref[0, i * tq:(i + 1) * tq, :] = (
                jnp.concatenate(o_heads, axis=0).T.astype(o_ref.dtype))

    for c in range(n):
        s_ref[c, :n_keys(c, 0, 0)] = logits(c, 0, 0)

    sequence = [(i, j) for i in range(n_tiles) for j in range(i + 1)]

    @pl.when(pl.program_id(0) >= 0)
    def _():
        for idx, (i, j) in enumerate(sequence):
            step(i, j, sequence[idx + 1] if idx + 1 < len(sequence) else None)


def _fox_attn(q, ka, vt, tq=512):
    b, s, d = q.shape
    qo = pl.BlockSpec((1, s, LANES), lambda bi, p: (bi, 0, p))
    return pl.pallas_call(
        functools.partial(_fox_attn_kernel, tq=tq, strip=MXU_DIM),
        grid=(b, d // LANES),
        in_specs=[qo, pl.BlockSpec((1, s, 2 * LANES), lambda bi, p: (bi, 0, p)),
                  pl.BlockSpec((1, LANES, s), lambda bi, p: (bi, p, 0))],
        out_specs=qo,
        out_shape=jax.ShapeDtypeStruct((b, s, d), BF16),
        scratch_shapes=[pltpu.VMEM((2, s, 2 * LANES), BF16),
                        pltpu.VMEM((2, 1, tq), F32),
                        pltpu.VMEM((2, HEAD_DIM + ONES_ROWS, tq), F32),
                        pltpu.VMEM((2 * tq // MXU_DIM, tq, MXU_DIM), F32)],
        compiler_params=_params(("arbitrary", "arbitrary")),
        name="fox_attn",
    )(q, ka, vt)


def _rope_kernel(pos_ref, invf_ref, cos_ref, sin_ref):
    ang = pos_ref[0].astype(F32) * invf_ref[...]
    d = lax.broadcasted_iota(jnp.int32, ang.shape, 1) % HEAD_DIM
    half = ROPE_DIM // 2
    cos_ref[0] = jnp.where(d < ROPE_DIM, jnp.cos(ang), 1.0)
    sn = jnp.sin(ang)
    sin_ref[0] = jnp.where(d < half, -sn, jnp.where(d < ROPE_DIM, sn, 0.0))


def _rope_tables(positions, tm=512):
    b, s = positions.shape
    half = ROPE_DIM // 2
    inv_freq = ROPE_THETA ** (-(jnp.arange(half, dtype=F32) * 2.0 / ROPE_DIM))
    d = jnp.arange(LANES) % HEAD_DIM
    invf = jnp.where(d < ROPE_DIM, inv_freq[d % half], 0.0).reshape(1, LANES)
    tab = pl.BlockSpec((1, tm, LANES), lambda i, j: (i, j, 0))
    return pl.pallas_call(
        _rope_kernel,
        grid=(b, s // tm),
        in_specs=[pl.BlockSpec((1, tm, 1), lambda i, j: (i, j, 0)),
                  pl.BlockSpec((1, LANES), lambda i, j: (0, 0))],
        out_specs=[tab, tab],
        out_shape=[jax.ShapeDtypeStruct((b, s, LANES), F32)] * 2,
        compiler_params=_params(("arbitrary", "arbitrary")),
        name="rope_tables",
    )(positions.reshape(b, s, 1), invf)


def _rope(x, cos, sin):
    half = ROPE_DIM // 2
    d = lax.broadcasted_iota(jnp.int32, (1, LANES), 1) % HEAD_DIM
    outs = []
    for c in range(x.shape[1] // LANES):
        xc = x[:, c * LANES:(c + 1) * LANES]
        partner = jnp.where(d < half, pltpu.roll(xc, LANES - half, 1), pltpu.roll(xc, half, 1))
        outs.append(xc * cos + partner * sin)
    return jnp.concatenate(outs, axis=1)


def _dil_in_kernel(x_ref, g_ref, sh_ref, sc_ref, w_ref, qg_ref, kg_ref, cos_ref, sin_ref,
                   *refs):
    out_refs = refs
    tm, d = x_ref.shape[1:]
    h = _norm_mod(x_ref[0], g_ref[...], sh_ref[0], sc_ref[0]).astype(BF16)
    new = lax.broadcasted_iota(jnp.int32, (tm, tm), 0)
    old = lax.broadcasted_iota(jnp.int32, (tm, tm), 1)
    for grp, (_, dilation) in enumerate(DIL_CONFIGS):
        n = tm // dilation
        if dilation == 1:
            h_g, cos, sin = h, cos_ref[0], sin_ref[0]
        else:
            perm = jnp.where(old == (new % n) * dilation + new // n, 1.0, 0.0).astype(BF16)
            h_g = _dot(perm, h).astype(BF16)
            strided = lambda ref: jnp.concatenate(
                [ref[0, pl.ds(r, n, stride=dilation), :] for r in range(dilation)], axis=0)
            cos, sin = strided(cos_ref), strided(sin_ref)
        base = grp * 3 * d
        q = _head_rms_norm(_dot(h_g, w_ref[:, base:base + d]), qg_ref[grp:grp + 1, :])
        k = _head_rms_norm(_dot(h_g, w_ref[:, base + d:base + 2 * d]), kg_ref[grp:grp + 1, :])
        vals = (_rope(q, cos, sin) * (LOG2E * HEAD_DIM ** -0.5), _rope(k, cos, sin),
                _dot(h_g, w_ref[:, base + 2 * d:base + 3 * d]))
        for out_ref, val in zip(out_refs[3 * grp:3 * grp + 3], vals):
            val = val.astype(BF16)
            for r in range(dilation):
                out_ref[0, r] = val[r * n:(r + 1) * n]


def _dil_in(x, g, shift, scale, w_in, q_g, k_g, cos, sin, tm=256):
    b, s, d = x.shape
    n_groups = len(DIL_CONFIGS)
    n_heads = d // HEAD_DIM
    qg = jnp.tile(q_g, (1, n_heads))
    kg = jnp.tile(k_g, (1, n_heads))
    row = pl.BlockSpec((1, 1, d), lambda i, j: (i, 0, 0))
    full = lambda shp: pl.BlockSpec(shp, lambda i, j: (0, 0))
    act = pl.BlockSpec((1, tm, d), lambda i, j: (i, j, 0))
    tab = pl.BlockSpec((1, tm, LANES), lambda i, j: (i, j, 0))
    w = w_in.astype(BF16)
    out_specs, out_shape = [], []
    for _, dil in DIL_CONFIGS:
        out_specs += [pl.BlockSpec((1, dil, tm // dil, d), lambda i, j: (i, 0, j, 0))] * 3
        out_shape += [jax.ShapeDtypeStruct((b, dil, s // dil, d), BF16)] * 3
    return pl.pallas_call(
        _dil_in_kernel,
        grid=(b, s // tm),
        in_specs=[act, full((1, d)), row, row, _resident(w.shape, lambda i, j: (0, 0)),
                  full((n_groups, d)), full((n_groups, d)), tab, tab],
        out_specs=out_specs,
        out_shape=out_shape,
        compiler_params=_params(("arbitrary", "arbitrary")),
        name="dil_in",
    )(x, g, shift, scale, w, qg, kg, cos, sin)


def _dil_attn_kernel(q_ref, kc_ref, kp_ref, vc_ref, vp_ref, o_ref, lse_ref, lse_t_ref):
    w = DIL_WINDOW_KEYS
    n_res, tq, d = q_ref.shape[1:]
    i = pl.program_id(2)
    lane = lax.broadcasted_iota(jnp.int32, (1, LANES), 1)
    key = lax.broadcasted_iota(jnp.int32, (2 * w, 2 * w), 0)
    qry = lax.broadcasted_iota(jnp.int32, (2 * w, 2 * w), 1) % w
    band = jnp.logical_and(key >= qry, key <= qry + w)
    bias_band = jnp.where(band, 0.0, -jnp.inf)
    in_seq = jnp.logical_or(key >= w, i > 0)
    bias_first = jnp.where(jnp.logical_and(band, in_seq), 0.0, -jnp.inf)
    dim = lax.broadcasted_iota(jnp.int32, (LANES, 1), 0)

    chains = [(res, p, sub) for res in range(n_res) for p in range(d // LANES)
              for sub in range(tq // w)]
    head_a = jnp.where(lane < HEAD_DIM, 1.0, 0.0).astype(BF16)
    head_b = jnp.where(lane >= HEAD_DIM, 1.0, 0.0).astype(BF16)
    ones_rows = jnp.ones((ONES_ROWS, 2 * w), BF16)

    def window(cur_ref, prev_ref, res, p, sub):
        cs = slice(p * LANES, (p + 1) * LANES)
        if sub == 0:
            return jnp.concatenate([prev_ref[0, res, :, cs], cur_ref[0, res, :w, cs]], axis=0)
        return cur_ref[0, res, (sub - 1) * w:(sub + 1) * w, cs]

    def logits(c):
        res, p, sub = chains[c]
        qs = q_ref[0, res, sub * w:(sub + 1) * w, p * LANES:(p + 1) * LANES]
        q_both = jnp.concatenate([qs * head_a, qs * head_b], axis=0)
        s = _dot_nt(window(kc_ref, kp_ref, res, p, sub), q_both)
        return s + (bias_first if sub == 0 else bias_band)

    def finish(c, s):
        res, p, sub = chains[c]
        rows = slice(sub * w, (sub + 1) * w)
        m = jnp.max(s, axis=0, keepdims=True)
        e = jnp.exp2(s - m)
        v_t = jnp.concatenate([window(vc_ref, vp_ref, res, p, sub).T, ones_rows], axis=0)
        o_l = _dot(v_t, e.astype(BF16))
        l = o_l[LANES:LANES + 1]
        o_t = o_l[:LANES] * (1.0 / l)
        o_ref[0, res, rows, p * LANES:(p + 1) * LANES] = (
            jnp.where(dim < HEAD_DIM, o_t[:, :w], o_t[:, w:]).T)
        lse = m + jnp.log(l) * LOG2E
        lse_t_ref[res, 2 * p:2 * p + 1, rows] = lse[:, :w]
        lse_t_ref[res, 2 * p + 1:2 * p + 2, rows] = lse[:, w:]

    lse_t_ref[...] = jnp.zeros_like(lse_t_ref)
    ahead = 4
    pending = [logits(c) for c in range(ahead)]
    for c in range(len(chains)):
        if c + ahead < len(chains):
            pending.append(logits(c + ahead))
        finish(c, pending[c])
    for res in range(n_res):
        lse_ref[0, res] = lse_t_ref[res].T


def _dil_attn(q, k, v, rows_per_step=512):
    b, dilation, sub_len, d = q.shape
    w = DIL_WINDOW_KEYS
    tq = min(rows_per_step, sub_len)
    n_res = min(rows_per_step // tq, dilation)
    cur = pl.BlockSpec((1, n_res, tq, d), lambda bi, r, i: (bi, r, i, 0))
    prev = pl.BlockSpec((1, n_res, w, d),
                        lambda bi, r, i: (bi, r, jnp.maximum(i * (tq // w) - 1, 0), 0))
    return pl.pallas_call(
        _dil_attn_kernel,
        grid=(b, dilation // n_res, sub_len // tq),
        in_specs=[cur, cur, prev, cur, prev],
        out_specs=[cur, pl.BlockSpec((1, n_res, tq, LANES), lambda bi, r, i: (bi, r, i, 0))],
        out_shape=[jax.ShapeDtypeStruct(q.shape, F32),
                   jax.ShapeDtypeStruct((b, dilation, sub_len, LANES), F32)],
        scratch_shapes=[pltpu.VMEM((n_res, LANES, tq), F32)],
        compiler_params=_params(("arbitrary", "arbitrary", "arbitrary")),
        name="dil_attn",
    )(q, k, k, v, v)


def _dil_merge(o_refs, l_refs, o_stage_ref, l_stage_ref):
    d = o_refs[0].shape[3]

    def token_order(ref, stage_ref, slot):
        dilation, n, width = ref.shape[1:]
        if dilation == 1:
            return ref[0, 0]
        n_tiles = width // LANES
        for r in range(dilation):
            for t in range(n_tiles):
                stage_ref[slot * n_tiles + t, pl.ds(r, n, stride=dilation), :] = (
                    ref[0, r, :, t * LANES:(t + 1) * LANES])
        return jnp.concatenate([stage_ref[slot * n_tiles + t] for t in range(n_tiles)], axis=1)

    lses = [token_order(ref, l_stage_ref, g) for g, ref in enumerate(l_refs)]
    outs = [token_order(ref, o_stage_ref, g) for g, ref in enumerate(o_refs)]
    m = jnp.maximum(jnp.maximum(lses[0], lses[1]), lses[2])
    es = [jnp.exp2(l - m) for l in lses]
    inv = 1.0 / (es[0] + es[1] + es[2])
    r = lax.broadcasted_iota(jnp.int32, (LANES, d), 0)
    c = lax.broadcasted_iota(jnp.int32, (LANES, d), 1) // HEAD_DIM
    expand = jnp.where(r == c, 1.0, 0.0).astype(BF16)
    merged = outs[-1]
    for e, o in zip(es[:-1], outs[:-1]):
        merged = merged + _dot((e * inv).astype(BF16), expand) * (o - outs[-1])
    return merged


def kernel(x, c, positions, mod_w, mod_b, norm_g, ffn_w_gate, ffn_w_up, ffn_w_down,
           fox_w_in, fox_b_f, fox_q_g, fox_k_g, fox_w_out,
           dil_w_in, dil_q_g, dil_k_g, dil_w_out):
    b, s, d = x.shape
    depth = mod_w.shape[0]
    mod = _modulation(c, mod_w, mod_b).reshape(depth, b, N_SUBLAYERS, 3, 1, d)
    wg = ffn_w_gate.astype(BF16)
    wu = ffn_w_up.astype(BF16)
    wd = ffn_w_down.astype(BF16)
    cos, sin = _rope_tables(positions)

    for i in range(depth):
        shift = lambda sub: mod[i, :, sub, 0]
        scale = lambda sub: mod[i, :, sub, 1]
        gate = lambda sub: mod[i, :, sub, 2]
        g = lambda sub: norm_g[i, sub].reshape(1, d)

        x = _ffn(x, g(0), shift(0), scale(0), gate(0), wg, wu, wd, i, 0)
        j = i // 2
        if i % 2 == 0:
            q, ka, vt = _fox_in(x, g(1), shift(1), scale(1), fox_w_in[j], fox_b_f[j],
                                fox_q_g[j], fox_k_g[j])
            mixer = (_fox_attn(q, ka, vt), fox_w_out[j], gate(1))
        else:
            qkv = _dil_in(x, g(1), shift(1), scale(1), dil_w_in[j], dil_q_g[j], dil_k_g[j],
                          cos, sin)
            outs, lses = [], []
            for grp in range(len(DIL_CONFIGS)):
                o, lse = _dil_attn(qkv[3 * grp], qkv[3 * grp + 1], qkv[3 * grp + 2])
                outs.append(o)
                lses.append(lse)
            mixer = (outs, lses, dil_w_out[j], gate(1))
        x = _ffn(x, g(2), shift(2), scale(2), gate(2), wg, wu, wd, i, 1, mixer=mixer)
    return x
```

```python
import functools

import jax
import jax.numpy as jnp
from jax import lax
from jax.experimental import pallas as pl
from jax.experimental.pallas import tpu as pltpu

HEAD_DIM = 64
ROPE_DIM = 16
ROPE_THETA = 500000.0
EPS = 1e-6
MACARON_WEIGHT = 0.5
N_SUBLAYERS = 3
DIL_CONFIGS = ((128, 1), (512, 4), (2048, 16))
DIL_WINDOW_KEYS = 128

LANES = 128
MXU_DIM = 256
VMEM_LIMIT = 56 * 1024 * 1024

F32 = jnp.float32
BF16 = jnp.bfloat16


def _params(semantics, vmem=VMEM_LIMIT):
    return pltpu.CompilerParams(dimension_semantics=semantics, vmem_limit_bytes=vmem)


def _resident(block_shape, index_map):
    return pl.BlockSpec(block_shape, index_map, pipeline_mode=pl.Buffered(1))


def _dot(a, b):
    return jnp.dot(a, b, preferred_element_type=F32)


def _dot_nt(a, b):
    return lax.dot_general(a, b, (((1,), (1,)), ((), ())), preferred_element_type=F32)


def _sigmoid(x):
    return 1.0 / (1.0 + jnp.exp(-x))


def _norm_mod(x, g, shift, scale):
    ms = jnp.mean(x * x, axis=-1, keepdims=True)
    y = x * lax.rsqrt(ms + EPS) * g
    return y * (1.0 + scale) + shift


def _head_block_ones(n):
    r = lax.broadcasted_iota(jnp.int32, (n, n), 0) // HEAD_DIM
    c = lax.broadcasted_iota(jnp.int32, (n, n), 1) // HEAD_DIM
    return jnp.where(r == c, 1.0, 0.0).astype(BF16)


def _head_rms_norm(x, gain):
    ones = _head_block_ones(MXU_DIM)
    outs = []
    for c in range(x.shape[1] // MXU_DIM):
        xc = x[:, c * MXU_DIM:(c + 1) * MXU_DIM]
        ssq = _dot((xc * xc).astype(BF16), ones)
        r = lax.rsqrt(ssq * (1.0 / HEAD_DIM) + EPS)
        outs.append(xc * r * gain[:, c * MXU_DIM:(c + 1) * MXU_DIM])
    return jnp.concatenate(outs, axis=1)


def _mod_kernel(c_ref, w_ref, b_ref, o_ref):
    c = c_ref[...]
    c_act = c * _sigmoid(c)
    w = w_ref[0]
    w_hi = w.astype(BF16)
    w_lo = (w - w_hi.astype(F32)).astype(BF16)
    c_hi = c_act.astype(BF16)
    c_lo = (c_act - c_hi.astype(F32)).astype(BF16)
    o_ref[0] = _dot(c_hi, w_hi) + _dot(c_lo, w_hi) + _dot(c_hi, w_lo) + b_ref[0]


def _modulation(c, mod_w, mod_b):
    depth, d, n = mod_w.shape
    b = c.shape[0]
    tn = 1024
    return pl.pallas_call(
        _mod_kernel,
        grid=(depth, n // tn),
        in_specs=[
            pl.BlockSpec((b, d), lambda i, j: (0, 0)),
            pl.BlockSpec((1, d, tn), lambda i, j: (i, 0, j)),
            pl.BlockSpec((1, 1, tn), lambda i, j: (i, 0, j)),
        ],
        out_specs=pl.BlockSpec((1, b, tn), lambda i, j: (i, 0, j)),
        out_shape=jax.ShapeDtypeStruct((depth, b, n), F32),
        compiler_params=_params(("arbitrary", "arbitrary")),
        name="modulation",
    )(c, mod_w, mod_b.reshape(depth, 1, n))


FFN_CHUNK = 256


def _swiglu_residual(x, g_ref, sh_ref, sc_ref, gt_ref, wg_ref, wu_ref, wd_ref):
    h = _norm_mod(x, g_ref[...], sh_ref[0], sc_ref[0]).astype(BF16)
    acc = jnp.zeros_like(x)
    for c in range(wg_ref.shape[1] // FFN_CHUNK):
        cs = slice(c * FFN_CHUNK, (c + 1) * FFN_CHUNK)
        gg = _dot(h, wg_ref[:, cs])
        uu = _dot(h, wu_ref[:, cs])
        a = (gg * _sigmoid(gg) * uu).astype(BF16)
        acc = acc + _dot(a, wd_ref[cs, :])
    return x + (MACARON_WEIGHT * gt_ref[0]) * acc


def _ffn_kernel(x_ref, *refs):
    ffn_refs, o_ref = refs[:-1], refs[-1]
    o_ref[0] = _swiglu_residual(x_ref[0], *ffn_refs)


def _proj_ffn_kernel(x_ref, a_ref, wo_ref, gm_ref, *refs):
    ffn_refs, o_ref = refs[:-1], refs[-1]
    x = x_ref[0] + gm_ref[0] * _dot(a_ref[0], wo_ref[...])
    o_ref[0] = _swiglu_residual(x, *ffn_refs)


def _dil_ffn_kernel(x_ref, o0_ref, o1_ref, o2_ref, l0_ref, l1_ref, l2_ref, wo_ref, gm_ref, *refs):
    ffn_refs, o_ref, o_stage_ref, l_stage_ref = refs[:-3], refs[-3], refs[-2], refs[-1]
    merged = _dil_merge((o0_ref, o1_ref, o2_ref), (l0_ref, l1_ref, l2_ref),
                        o_stage_ref, l_stage_ref)
    x = x_ref[0] + gm_ref[0] * _dot(merged.astype(BF16), wo_ref[...])
    o_ref[0] = _swiglu_residual(x, *ffn_refs)


def _ffn(x, g, shift, scale, gate, wg, wu, wd, layer, which, mixer=None):
    b, s, d = x.shape
    f = wg.shape[-1]
    tm = 512 if mixer is not None and len(mixer) == 4 else 1024
    row = pl.BlockSpec((1, 1, d), lambda i, j: (i, 0, 0))
    act = pl.BlockSpec((1, tm, d), lambda i, j: (i, j, 0))
    ffn_specs = [
        pl.BlockSpec((1, d), lambda i, j: (0, 0)),
        row, row, row,
        _resident((None, None, d, f), lambda i, j: (layer, which, 0, 0)),
        _resident((None, None, d, f), lambda i, j: (layer, which, 0, 0)),
        _resident((None, None, f, d), lambda i, j: (layer, which, 0, 0)),
    ]
    ffn_args = (g, shift, scale, gate, wg, wu, wd)
    scratch = []
    if mixer is None:
        body, mix_specs, mix_args = _ffn_kernel, [], ()
    elif len(mixer) == 3:
        o, w_out, gate_m = mixer
        body = _proj_ffn_kernel
        mix_specs = [act, _resident(w_out.shape, lambda i, j: (0, 0)), row]
        mix_args = (o, w_out.astype(BF16), gate_m)
    else:
        outs, lses, w_out, gate_m = mixer
        body = _dil_ffn_kernel
        grouped = lambda a: pl.BlockSpec((1, a.shape[1], tm // a.shape[1], a.shape[3]),
                                         lambda i, j: (i, 0, j, 0))
        mix_specs = ([grouped(a) for a in outs] + [grouped(a) for a in lses]
                     + [_resident(w_out.shape, lambda i, j: (0, 0)), row])
        mix_args = (*outs, *lses, w_out.astype(BF16), gate_m)
        scratch = [pltpu.VMEM((len(outs) * d // LANES, tm, LANES), F32),
                   pltpu.VMEM((len(lses), tm, LANES), F32)]
    return pl.pallas_call(
        body,
        grid=(b, s // tm),
        in_specs=[act] + mix_specs + ffn_specs,
        out_specs=act,
        out_shape=jax.ShapeDtypeStruct(x.shape, x.dtype),
        scratch_shapes=scratch,
        compiler_params=_params(("arbitrary", "arbitrary")),
        name="ffn",
    )(x, *mix_args, *ffn_args)


def _cumsum_rows(x):
    n = x.shape[0]
    row = lax.broadcasted_iota(jnp.int32, x.shape, 0)
    shift = 1
    while shift < n:
        x = x + jnp.where(row >= shift, pltpu.roll(x, shift, 0), 0.0)
        shift *= 2
    return x


N_SPLIT = 3
LOG2E = 1.4426950408889634


def _fox_in_kernel(x_ref, g_ref, sh_ref, sc_ref, wqk_ref, wvt_ref, wf_ref, bf_ref, qg_ref,
                   kg_ref, sel_ref, q_ref, ka_ref, vt_ref, carry_ref, *, n_heads):
    d = x_ref.shape[2]
    tm = x_ref.shape[1]

    @pl.when(pl.program_id(1) == 0)
    def _():
        carry_ref[...] = jnp.zeros_like(carry_ref)

    h = _norm_mod(x_ref[0], g_ref[...], sh_ref[0], sc_ref[0]).astype(BF16)
    z = _dot(h, wf_ref[...]) + bf_ref[...]
    logf = jnp.minimum(z, 0.0) - jnp.log1p(jnp.exp(-jnp.abs(z)))
    lane = lax.broadcasted_iota(jnp.int32, logf.shape, 1)
    logf = jnp.where(lane < n_heads, logf, 0.0)
    cum = _cumsum_rows(logf) + carry_ref[...]
    carry_ref[...] = cum[tm - 1:tm, :]

    q = _head_rms_norm(_dot(h, wqk_ref[:, 0:d]), qg_ref[...])
    q_ref[0] = (q * (LOG2E * HEAD_DIM ** -0.5)).astype(BF16)
    k = _head_rms_norm(_dot(h, wqk_ref[:, d:2 * d]), kg_ref[...]).astype(BF16)
    vt_ref[0] = _dot_nt(wvt_ref[...], h).astype(BF16)

    terms = []
    rest = cum * LOG2E
    for _ in range(N_SPLIT):
        t = rest.astype(BF16)
        terms.append(t)
        rest = rest - t.astype(F32)
    kx = _dot(jnp.concatenate(terms, axis=1), sel_ref[...]).astype(BF16)
    pieces = []
    for p in range(d // LANES):
        pieces += [k[:, p * LANES:(p + 1) * LANES], kx[:, p * LANES:(p + 1) * LANES]]
    ka_ref[0] = jnp.concatenate(pieces, axis=1)


def _fox_in(x, g, shift, scale, w_in, b_f, q_g, k_g, tm=1024):
    b, s, d = x.shape
    n_heads = d // HEAD_DIM
    wqk = w_in[:, :2 * d].astype(BF16)
    wvt = w_in[:, 2 * d:3 * d].T.astype(BF16)
    wf = jnp.pad(w_in[:, 3 * d:], ((0, 0), (0, LANES - n_heads))).astype(BF16)
    bf = jnp.pad(b_f, (0, LANES - n_heads)).reshape(1, LANES)
    qg = jnp.tile(q_g, n_heads).reshape(1, d)
    kg = jnp.tile(k_g, n_heads).reshape(1, d)
    heads = jnp.arange(n_heads)
    sel = jnp.zeros((N_SPLIT * LANES, d), BF16)
    for t in range(N_SPLIT):
        sel = sel.at[t * LANES + heads, (heads // 2) * LANES + (heads % 2) * N_SPLIT + t].set(1.0)
    row = pl.BlockSpec((1, 1, d), lambda i, j: (i, 0, 0))
    full = lambda shp: pl.BlockSpec(shp, lambda i, j: (0, 0))
    res = lambda a: _resident(a.shape, lambda i, j: (0, 0))
    act = pl.BlockSpec((1, tm, d), lambda i, j: (i, j, 0))
    return pl.pallas_call(
        functools.partial(_fox_in_kernel, n_heads=n_heads),
        grid=(b, s // tm),
        in_specs=[act, full((1, d)), row, row, res(wqk), res(wvt), res(wf),
                  full((1, LANES)), full((1, d)), full((1, d)), res(sel)],
        out_specs=[act, pl.BlockSpec((1, tm, 2 * d), lambda i, j: (i, j, 0)),
                   pl.BlockSpec((1, d, tm), lambda i, j: (i, 0, j))],
        out_shape=[jax.ShapeDtypeStruct((b, s, d), BF16),
                   jax.ShapeDtypeStruct((b, s, 2 * d), BF16),
                   jax.ShapeDtypeStruct((b, d, s), BF16)],
        scratch_shapes=[pltpu.VMEM((1, LANES), F32)],
        compiler_params=_params(("arbitrary", "arbitrary")),
        name="fox_in",
    )(x, g, shift, scale, wqk, wvt, wf, bf, qg, kg, sel)


ONES_ROWS = 16


def _fox_attn_kernel(q_ref, ka_ref, vt_ref, o_ref, qa_ref, m_ref, acc_ref, s_ref, *, tq, strip):
    tk = tq
    n_tiles = q_ref.shape[1] // tq
    lane = lax.broadcasted_iota(jnp.int32, (1, LANES), 1)
    q2 = q_ref[0]
    zero = jnp.zeros_like(q2)
    for hh in range(2):
        head = (lane < HEAD_DIM) if hh == 0 else (lane >= HEAD_DIM)
        pick = jnp.logical_and(lane >= hh * N_SPLIT, lane < (hh + 1) * N_SPLIT)
        minus_one = jnp.broadcast_to(jnp.where(pick, -1.0, 0.0), q2.shape).astype(BF16)
        qa_ref[hh] = jnp.concatenate([jnp.where(head, q2, zero), minus_one], axis=1)

    ones_rows = jnp.ones((ONES_ROWS, tk), BF16)
    chains = [(hh, slice(st * strip, (st + 1) * strip), st)
              for st in range(tq // strip) for hh in range(2)]
    n = len(chains)

    def n_keys(c, i, j):
        return (chains[c][2] + 1) * strip if i == j else tk

    def logits(c, i, j):
        hh, _, st = chains[c]
        keys = slice(j * tk, j * tk + n_keys(c, i, j))
        qrys = slice(i * tq + st * strip, i * tq + (st + 1) * strip)
        s = _dot_nt(ka_ref[0, keys, :], qa_ref[hh, qrys, :])
        if i == j:
            key = lax.broadcasted_iota(jnp.int32, s.shape, 0)
            qry = lax.broadcasted_iota(jnp.int32, s.shape, 1) + st * strip
            s = jnp.where(key <= qry, s, -jnp.inf)
        return s

    def softmax_pv(c, i, j):
        hh, cols, _ = chains[c]
        nk = n_keys(c, i, j)
        s = s_ref[c, :nk]
        m_prev = m_ref[hh, :, cols]
        m_new = jnp.maximum(m_prev, jnp.max(s, axis=0, keepdims=True))
        p = jnp.exp2(s - m_new)
        alpha = jnp.exp2(m_prev - m_new)
        m_ref[hh, :, cols] = m_new
        v_t = vt_ref[0, hh * HEAD_DIM:(hh + 1) * HEAD_DIM, j * tk:j * tk + nk]
        vt_ones = jnp.concatenate([v_t, ones_rows[:, :nk]], axis=0)
        acc_ref[hh, :, cols] = alpha * acc_ref[hh, :, cols] + _dot(vt_ones, p.astype(BF16))

    def step(i, j, nxt):
        if j == 0:
            m_ref[...] = jnp.full_like(m_ref, -jnp.inf)
            acc_ref[...] = jnp.zeros_like(acc_ref)
        ahead = 2
        s_next = []
        for c in range(n + ahead):
            if nxt is not None and c < n:
                s_next.append(logits(c, *nxt))
            if c >= ahead:
                softmax_pv(c - ahead, i, j)
                if nxt is not None:
                    s_ref[c - ahead, :n_keys(c - ahead, *nxt)] = s_next[c - ahead]
        if j == i:
            o_heads = [acc_ref[hh, :HEAD_DIM, :] * (1.0 / acc_ref[hh, HEAD_DIM:HEAD_DIM + 1, :])
                       for hh in range(2)]
            o_ref[0, i * tq:(i + 1) * tq, :] = (
                jnp.concatenate(o_heads, axis=0).T.astype(o_ref.dtype))

    for c in range(n):
        s_ref[c, :n_keys(c, 0, 0)] = logits(c, 0, 0)

    sequence = [(i, j) for i in range(n_tiles) for j in range(i + 1)]

    @pl.when(pl.program_id(0) >= 0)
    def _():
        for idx, (i, j) in enumerate(sequence):
            step(i, j, sequence[idx + 1] if idx + 1 < len(sequence) else None)


def _fox_attn(q, ka, vt, tq=512):
    b, s, d = q.shape
    qo = pl.BlockSpec((1, s, LANES), lambda bi, p: (bi, 0, p))
    return pl.pallas_call(
        functools.partial(_fox_attn_kernel, tq=tq, strip=MXU_DIM),
        grid=(b, d // LANES),
        in_specs=[qo, pl.BlockSpec((1, s, 2 * LANES), lambda bi, p: (bi, 0, p)),
                  pl.BlockSpec((1, LANES, s), lambda bi, p: (bi, p, 0))],
        out_specs=qo,
        out_shape=jax.ShapeDtypeStruct((b, s, d), BF16),
        scratch_shapes=[pltpu.VMEM((2, s, 2 * LANES), BF16),
                        pltpu.VMEM((2, 1, tq), F32),
                        pltpu.VMEM((2, HEAD_DIM + ONES_ROWS, tq), F32),
                        pltpu.VMEM((2 * tq // MXU_DIM, tq, MXU_DIM), F32)],
        compiler_params=_params(("arbitrary", "arbitrary")),
        name="fox_attn",
    )(q, ka, vt)


def _rope_kernel(pos_ref, invf_ref, cos_ref, sin_ref):
    ang = pos_ref[0].astype(F32) * invf_ref[...]
    d = lax.broadcasted_iota(jnp.int32, ang.shape, 1) % HEAD_DIM
    half = ROPE_DIM // 2
    cos_ref[0] = jnp.where(d < ROPE_DIM, jnp.cos(ang), 1.0)
    sn = jnp.sin(ang)
    sin_ref[0] = jnp.where(d < half, -sn, jnp.where(d < ROPE_DIM, sn, 0.0))


def _rope_tables(positions, tm=512):
    b, s = positions.shape
    half = ROPE_DIM // 2
    inv_freq = ROPE_THETA ** (-(jnp.arange(half, dtype=F32) * 2.0 / ROPE_DIM))
    d = jnp.arange(LANES) % HEAD_DIM
    invf = jnp.where(d < ROPE_DIM, inv_freq[d % half], 0.0).reshape(1, LANES)
    tab = pl.BlockSpec((1, tm, LANES), lambda i, j: (i, j, 0))
    return pl.pallas_call(
        _rope_kernel,
        grid=(b, s // tm),
        in_specs=[pl.BlockSpec((1, tm, 1), lambda i, j: (i, j, 0)),
                  pl.BlockSpec((1, LANES), lambda i, j: (0, 0))],
        out_specs=[tab, tab],
        out_shape=[jax.ShapeDtypeStruct((b, s, LANES), F32)] * 2,
        compiler_params=_params(("arbitrary", "arbitrary")),
        name="rope_tables",
    )(positions.reshape(b, s, 1), invf)


def _rope(x, cos, sin):
    half = ROPE_DIM // 2
    d = lax.broadcasted_iota(jnp.int32, (1, LANES), 1) % HEAD_DIM
    outs = []
    for c in range(x.shape[1] // LANES):
        xc = x[:, c * LANES:(c + 1) * LANES]
        partner = jnp.where(d < half, pltpu.roll(xc, LANES - half, 1), pltpu.roll(xc, half, 1))
        outs.append(xc * cos + partner * sin)
    return jnp.concatenate(outs, axis=1)


def _dil_in_kernel(x_ref, g_ref, sh_ref, sc_ref, w_ref, qg_ref, kg_ref, cos_ref, sin_ref,
                   *refs):
    out_refs = refs
    tm, d = x_ref.shape[1:]
    h = _norm_mod(x_ref[0], g_ref[...], sh_ref[0], sc_ref[0]).astype(BF16)
    new = lax.broadcasted_iota(jnp.int32, (tm, tm), 0)
    old = lax.broadcasted_iota(jnp.int32, (tm, tm), 1)
    for grp, (_, dilation) in enumerate(DIL_CONFIGS):
        n = tm // dilation
        if dilation == 1:
            h_g, cos, sin = h, cos_ref[0], sin_ref[0]
        else:
            perm = jnp.where(old == (new % n) * dilation + new // n, 1.0, 0.0).astype(BF16)
            h_g = _dot(perm, h).astype(BF16)
            strided = lambda ref: jnp.concatenate(
                [ref[0, pl.ds(r, n, stride=dilation), :] for r in range(dilation)], axis=0)
            cos, sin = strided(cos_ref), strided(sin_ref)
        base = grp * 3 * d
        q = _head_rms_norm(_dot(h_g, w_ref[:, base:base + d]), qg_ref[grp:grp + 1, :])
        k = _head_rms_norm(_dot(h_g, w_ref[:, base + d:base + 2 * d]), kg_ref[grp:grp + 1, :])
        vals = (_rope(q, cos, sin) * (LOG2E * HEAD_DIM ** -0.5), _rope(k, cos, sin),
                _dot(h_g, w_ref[:, base + 2 * d:base + 3 * d]))
        for out_ref, val in zip(out_refs[3 * grp:3 * grp + 3], vals):
            val = val.astype(BF16)
            for r in range(dilation):
                out_ref[0, r] = val[r * n:(r + 1) * n]


def _dil_in(x, g, shift, scale, w_in, q_g, k_g, cos, sin, tm=256):
    b, s, d = x.shape
    n_groups = len(DIL_CONFIGS)
    n_heads = d // HEAD_DIM
    qg = jnp.tile(q_g, (1, n_heads))
    kg = jnp.tile(k_g, (1, n_heads))
    row = pl.BlockSpec((1, 1, d), lambda i, j: (i, 0, 0))
    full = lambda shp: pl.BlockSpec(shp, lambda i, j: (0, 0))
    act = pl.BlockSpec((1, tm, d), lambda i, j: (i, j, 0))
    tab = pl.BlockSpec((1, tm, LANES), lambda i, j: (i, j, 0))
    w = w_in.astype(BF16)
    out_specs, out_shape = [], []
    for _, dil in DIL_CONFIGS:
        out_specs += [pl.BlockSpec((1, dil, tm // dil, d), lambda i, j: (i, 0, j, 0))] * 3
        out_shape += [jax.ShapeDtypeStruct((b, dil, s // dil, d), BF16)] * 3
    return pl.pallas_call(
        _dil_in_kernel,
        grid=(b, s // tm),
        in_specs=[act, full((1, d)), row, row, _resident(w.shape, lambda i, j: (0, 0)),
                  full((n_groups, d)), full((n_groups, d)), tab, tab],
        out_specs=out_specs,
        out_shape=out_shape,
        compiler_params=_params(("arbitrary", "arbitrary")),
        name="dil_in",
    )(x, g, shift, scale, w, qg, kg, cos, sin)


def _dil_attn_kernel(q_ref, kc_ref, kp_ref, vc_ref, vp_ref, o_ref, lse_ref, lse_t_ref):
    w = DIL_WINDOW_KEYS
    n_res, tq, d = q_ref.shape[1:]
    i = pl.program_id(2)
    lane = lax.broadcasted_iota(jnp.int32, (1, LANES), 1)
    key = lax.broadcasted_iota(jnp.int32, (2 * w, 2 * w), 0)
    qry = lax.broadcasted_iota(jnp.int32, (2 * w, 2 * w), 1) % w
    band = jnp.logical_and(key >= qry, key <= qry + w)
    bias_band = jnp.where(band, 0.0, -jnp.inf)
    in_seq = jnp.logical_or(key >= w, i > 0)
    bias_first = jnp.where(jnp.logical_and(band, in_seq), 0.0, -jnp.inf)
    dim = lax.broadcasted_iota(jnp.int32, (LANES, 1), 0)

    chains = [(res, p, sub) for res in range(n_res) for p in range(d // LANES)
              for sub in range(tq // w)]
    head_a = jnp.where(lane < HEAD_DIM, 1.0, 0.0).astype(BF16)
    head_b = jnp.where(lane >= HEAD_DIM, 1.0, 0.0).astype(BF16)
    ones_rows = jnp.ones((ONES_ROWS, 2 * w), BF16)

    def window(cur_ref, prev_ref, res, p, sub):
        cs = slice(p * LANES, (p + 1) * LANES)
        if sub == 0:
            return jnp.concatenate([prev_ref[0, res, :, cs], cur_ref[0, res, :w, cs]], axis=0)
        return cur_ref[0, res, (sub - 1) * w:(sub + 1) * w, cs]

    def logits(c):
        res, p, sub = chains[c]
        qs = q_ref[0, res, sub * w:(sub + 1) * w, p * LANES:(p + 1) * LANES]
        q_both = jnp.concatenate([qs * head_a, qs * head_b], axis=0)
        s = _dot_nt(window(kc_ref, kp_ref, res, p, sub), q_both)
        return s + (bias_first if sub == 0 else bias_band)

    def finish(c, s):
        res, p, sub = chains[c]
        rows = slice(sub * w, (sub + 1) * w)
        m = jnp.max(s, axis=0, keepdims=True)
        e = jnp.exp2(s - m)
        v_t = jnp.concatenate([window(vc_ref, vp_ref, res, p, sub).T, ones_rows], axis=0)
        o_l = _dot(v_t, e.astype(BF16))
        l = o_l[LANES:LANES + 1]
        o_t = o_l[:LANES] * (1.0 / l)
        o_ref[0, res, rows, p * LANES:(p + 1) * LANES] = (
            jnp.where(dim < HEAD_DIM, o_t[:, :w], o_t[:, w:]).T)
        lse = m + jnp.log(l) * LOG2E
        lse_t_ref[res, 2 * p:2 * p + 1, rows] = lse[:, :w]
        lse_t_ref[res, 2 * p + 1:2 * p + 2, rows] = lse[:, w:]

    lse_t_ref[...] = jnp.zeros_like(lse_t_ref)
    ahead = 4
    pending = [logits(c) for c in range(ahead)]
    for c in range(len(chains)):
        if c + ahead < len(chains):
            pending.append(logits(c + ahead))
        finish(c, pending[c])
    for res in range(n_res):
        lse_ref[0, res] = lse_t_ref[res].T


def _dil_attn(q, k, v, rows_per_step=512):
    b, dilation, sub_len, d = q.shape
    w = DIL_WINDOW_KEYS
    tq = min(rows_per_step, sub_len)
    n_res = min(rows_per_step // tq, dilation)
    cur = pl.BlockSpec((1, n_res, tq, d), lambda bi, r, i: (bi, r, i, 0))
    prev = pl.BlockSpec((1, n_res, w, d),
                        lambda bi, r, i: (bi, r, jnp.maximum(i * (tq // w) - 1, 0), 0))
    return pl.pallas_call(
        _dil_attn_kernel,
        grid=(b, dilation // n_res, sub_len // tq),
        in_specs=[cur, cur, prev, cur, prev],
        out_specs=[cur, pl.BlockSpec((1, n_res, tq, LANES), lambda bi, r, i: (bi, r, i, 0))],
        out_shape=[jax.ShapeDtypeStruct(q.shape, F32),
                   jax.ShapeDtypeStruct((b, dilation, sub_len, LANES), F32)],
        scratch_shapes=[pltpu.VMEM((n_res, LANES, tq), F32)],
        compiler_params=_params(("arbitrary", "arbitrary", "arbitrary")),
        name="dil_attn",
    )(q, k, k, v, v)


def _dil_merge(o_refs, l_refs, o_stage_ref, l_stage_ref):
    d = o_refs[0].shape[3]

    def token_order(ref, stage_ref, slot):
        dilation, n, width = ref.shape[1:]
        if dilation == 1:
            return ref[0, 0]
        n_tiles = width // LANES
        for r in range(dilation):
            for t in range(n_tiles):
                stage_ref[slot * n_tiles + t, pl.ds(r, n, stride=dilation), :] = (
                    ref[0, r, :, t * LANES:(t + 1) * LANES])
        return jnp.concatenate([stage_ref[slot * n_tiles + t] for t in range(n_tiles)], axis=1)

    lses = [token_order(ref, l_stage_ref, g) for g, ref in enumerate(l_refs)]
    outs = [token_order(ref, o_stage_ref, g) for g, ref in enumerate(o_refs)]
    m = jnp.maximum(jnp.maximum(lses[0], lses[1]), lses[2])
    es = [jnp.exp2(l - m) for l in lses]
    inv = 1.0 / (es[0] + es[1] + es[2])
    r = lax.broadcasted_iota(jnp.int32, (LANES, d), 0)
    c = lax.broadcasted_iota(jnp.int32, (LANES, d), 1) // HEAD_DIM
    expand = jnp.where(r == c, 1.0, 0.0).astype(BF16)
    merged = outs[-1]
    for e, o in zip(es[:-1], outs[:-1]):
        merged = merged + _dot((e * inv).astype(BF16), expand) * (o - outs[-1])
    return merged


def kernel(x, c, positions, mod_w, mod_b, norm_g, ffn_w_gate, ffn_w_up, ffn_w_down,
           fox_w_in, fox_b_f, fox_q_g, fox_k_g, fox_w_out,
           dil_w_in, dil_q_g, dil_k_g, dil_w_out):
    b, s, d = x.shape
    depth = mod_w.shape[0]
    mod = _modulation(c, mod_w, mod_b).reshape(depth, b, N_SUBLAYERS, 3, 1, d)
    wg = ffn_w_gate.astype(BF16)
    wu = ffn_w_up.astype(BF16)
    wd = ffn_w_down.astype(BF16)
    cos, sin = _rope_tables(positions)

    for i in range(depth):
        shift = lambda sub: mod[i, :, sub, 0]
        scale = lambda sub: mod[i, :, sub, 1]
        gate = lambda sub: mod[i, :, sub, 2]
        g = lambda sub: norm_g[i, sub].reshape(1, d)

        x = _ffn(x, g(0), shift(0), scale(0), gate(0), wg, wu, wd, i, 0)
        j = i // 2
        if i % 2 == 0:
            q, ka, vt = _fox_in(x, g(1), shift(1), scale(1), fox_w_in[j], fox_b_f[j],
                                fox_q_g[j], fox_k_g[j])
            mixer = (_fox_attn(q, ka, vt), fox_w_out[j], gate(1))
        else:
            qkv = _dil_in(x, g(1), shift(1), scale(1), dil_w_in[j], dil_q_g[j], dil_k_g[j],
                          cos, sin)
            outs, lses = [], []
            for grp in range(len(DIL_CONFIGS)):
                o, lse = _dil_attn(qkv[3 * grp], qkv[3 * grp + 1], qkv[3 * grp + 2])
                outs.append(o)
                lses.append(lse)
            mixer = (outs, lses, dil_w_out[j], gate(1))
        x = _ffn(x, g(2), shift(2), scale(2), gate(2), wg, wu, wd, i, 1, mixer=mixer)
    return x
```
